```python
import functools
import jax
import jax.numpy as jnp
from jax import lax
import numpy as np

D_MODEL = 1024
BATCH = 4
SEQ = 8192
DEPTH = 4

GRID_W = 64
CTX_LEN = 256
N_MIXERS = 3
EPS = 1e-6
NEG_INF = -1e30
ROPE_BASE = 10000.0
ADA_STD = 0.02

A_HEADS = 16
A_KV_HEADS = 4
A_HEAD_DIM = 64
A_WINDOW = 128
A_BLOCK = 128
A_WIDTH = A_HEADS * A_HEAD_DIM
A_KV_WIDTH = A_KV_HEADS * A_HEAD_DIM
A_IN = 2 * A_WIDTH + 2 * A_KV_WIDTH

B_HEADS = 8
B_NOPE = 128
B_ROPE = 64
B_VDIM = 128
B_Q_RANK = 256
B_KV_RANK = 128
B_BLOCK = 128
B_WIDTH = B_HEADS * B_VDIM
B_IN = B_Q_RANK + B_KV_RANK + B_ROPE + B_WIDTH

C_HEADS = 4
C_QK_DIM = 128
C_V_DIM = 256
C_QK_WIDTH = C_HEADS * C_QK_DIM
C_WIDTH = C_HEADS * C_V_DIM
C_CONV = 5
C_CHUNK = 128
C_IN = 2 * C_QK_WIDTH + 3 * C_WIDTH + 4 * C_HEADS
F_BIAS = 3.0

kernel_name = 'hybrid_dit_gqa_mla_mlstm'


def rmsnorm(x, g):
    xf = x.astype(jnp.float32)
    y = xf * lax.rsqrt(jnp.mean(xf * xf, axis=-1, keepdims=True) + EPS)
    return (y * g.astype(jnp.float32)).astype(x.dtype)


def axial_rope(n_tok, rot_dim, dtype):
    rows = n_tok // GRID_W
    r, cidx = jnp.meshgrid(jnp.arange(rows), jnp.arange(GRID_W), indexing='ij')
    r = r.reshape(-1).astype(jnp.float32)
    cc = cidx.reshape(-1).astype(jnp.float32)
    n_freq = rot_dim // 4
    inv = ROPE_BASE ** (-jnp.arange(n_freq, dtype=jnp.float32) / n_freq)
    ang = jnp.concatenate([r[:, None] * inv, cc[:, None] * inv], axis=-1)
    return jnp.cos(ang).astype(dtype), jnp.sin(ang).astype(dtype)


def apply_rope(x, cos, sin):
    x1, x2 = x[..., 0::2], x[..., 1::2]
    cos, sin = cos[:, None, :], sin[:, None, :]
    return jnp.stack([x1 * cos - x2 * sin, x1 * sin + x2 * cos], axis=-1).reshape(x.shape)


def sink_softmax(logits, sink):
    m = jnp.maximum(jnp.max(logits, axis=-1, keepdims=True), sink)
    e = jnp.exp(logits - m)
    return e / (jnp.sum(e, axis=-1, keepdims=True) + jnp.exp(sink - m))


def centred_dwconv(x, w):
    k = w.shape[0]
    return lax.conv_general_dilated(x, w[:, None, :].astype(x.dtype), window_strides=(1,),
                                    padding=[(k // 2, k // 2)],
                                    dimension_numbers=('NWC', 'WIO', 'NWC'),
                                    feature_group_count=x.shape[-1])


def window_gqa_mixer(h_lat, h_ctx, need_ctx, w_in, sink, w_out):
    bsz, n_lat, _ = h_lat.shape
    n_ctx = h_ctx.shape[1]
    grp = A_HEADS // A_KV_HEADS
    scale = A_HEAD_DIM ** -0.5
    cuts = [A_WIDTH, A_WIDTH + A_KV_WIDTH, A_WIDTH + 2 * A_KV_WIDTH]
    sink_l = sink.astype(jnp.float32).reshape(A_KV_HEADS, grp, 1, 1)

    q_l, k_l, v_l, z_l = jnp.split(h_lat @ w_in, cuts, axis=-1)
    if need_ctx:
        q_c, k_c, v_c, z_c = jnp.split(h_ctx @ w_in, cuts, axis=-1)
    else:
        k_c, v_c = jnp.split(h_ctx @ w_in[:, A_WIDTH:A_WIDTH + 2 * A_KV_WIDTH], 2, axis=-1)
    k_c = k_c.reshape(bsz, n_ctx, A_KV_HEADS, A_HEAD_DIM)
    v_c = v_c.reshape(bsz, n_ctx, A_KV_HEADS, A_HEAD_DIM)

    cos, sin = axial_rope(n_lat, A_HEAD_DIM, h_lat.dtype)
    q_l = apply_rope(q_l.reshape(bsz, n_lat, A_HEADS, A_HEAD_DIM), cos, sin)
    q_l = q_l.reshape(bsz, n_lat, A_KV_HEADS, grp, A_HEAD_DIM)
    k_l = apply_rope(k_l.reshape(bsz, n_lat, A_KV_HEADS, A_HEAD_DIM), cos, sin)
    v_l = v_l.reshape(bsz, n_lat, A_KV_HEADS, A_HEAD_DIM)

    pad = ((0, 0), (A_BLOCK, A_BLOCK), (0, 0), (0, 0))
    k_pad, v_pad = jnp.pad(k_l, pad), jnp.pad(v_l, pad)
    band = 3 * A_BLOCK
    k_off = jnp.arange(band) - A_BLOCK
    in_window = jnp.abs(k_off[None, :] - jnp.arange(A_BLOCK)[:, None]) <= A_WINDOW
    ctx_ok = jnp.ones((A_BLOCK, n_ctx), dtype=bool)

    def block(n):
        start = n * A_BLOCK
        q_b = lax.dynamic_slice_in_dim(q_l, start, A_BLOCK, axis=1)
        k_b = jnp.concatenate([lax.dynamic_slice_in_dim(k_pad, start, band, axis=1), k_c], axis=1)
        v_b = jnp.concatenate([lax.dynamic_slice_in_dim(v_pad, start, band, axis=1), v_c], axis=1)
        kpos = start + k_off
        valid = in_window & ((kpos >= 0) & (kpos < n_lat))[None, :]
        mask = jnp.concatenate([valid, ctx_ok], axis=1)
        logits = jnp.einsum('bqkgd,bjkd->bkgqj', q_b, k_b).astype(jnp.float32) * scale
        p = sink_softmax(jnp.where(mask, logits, NEG_INF), sink_l)
        return jnp.einsum('bkgqj,bjkd->bqkgd', p.astype(v_b.dtype), v_b)

    o = lax.map(block, jnp.arange(n_lat // A_BLOCK))
    o_l = jnp.moveaxis(o, 0, 1).reshape(bsz, n_lat, A_WIDTH)
    y_l = (o_l * jax.nn.silu(z_l)) @ w_out

    y_c = None
    if need_ctx:
        q_c = q_c.reshape(bsz, n_ctx, A_KV_HEADS, grp, A_HEAD_DIM)
        logits = jnp.einsum('bqkgd,bjkd->bkgqj', q_c, k_c).astype(jnp.float32) * scale
        p = sink_softmax(logits, sink_l)
        o_c = jnp.einsum('bkgqj,bjkd->bqkgd', p.astype(v_c.dtype), v_c).reshape(bsz, n_ctx, A_WIDTH)
        y_c = (o_c * jax.nn.silu(z_c)) @ w_out
    return y_l, y_c


def mla_mixer(h_lat, h_ctx, need_ctx, w_in, g_qa, g_kva, w_uq, w_ukv, w_out):
    bsz, n_lat, _ = h_lat.shape
    n_ctx = h_ctx.shape[1]
    dqk = B_NOPE + B_ROPE
    scale = dqk ** -0.5
    cuts = [B_Q_RANK, B_Q_RANK + B_KV_RANK, B_Q_RANK + B_KV_RANK + B_ROPE]

    def queries(q_a):
        b, n, _ = q_a.shape
        return (rmsnorm(q_a, g_qa) @ w_uq).reshape(b, n, B_HEADS, dqk)

    def keys_values(kv_a, k_pe):
        b, n, _ = kv_a.shape
        kv = (rmsnorm(kv_a, g_kva) @ w_ukv).reshape(b, n, B_HEADS, B_NOPE + B_VDIM)
        k_nope, v = jnp.split(kv, [B_NOPE], axis=-1)
        k_pe = jnp.broadcast_to(k_pe[:, :, None, :], (b, n, B_HEADS, B_ROPE))
        return jnp.concatenate([k_nope, k_pe], axis=-1), v

    q_a_l, kv_a_l, kpe_l, z_l = jnp.split(h_lat @ w_in, cuts, axis=-1)
    if need_ctx:
        q_a_c, kv_a_c, kpe_c, z_c = jnp.split(h_ctx @ w_in, cuts, axis=-1)
    else:
        kv_a_c, kpe_c = jnp.split(h_ctx @ w_in[:, B_Q_RANK:cuts[2]], [B_KV_RANK], axis=-1)

    cos, sin = axial_rope(n_lat, B_ROPE, h_lat.dtype)
    q_l = queries(q_a_l)
    q_l = jnp.concatenate([q_l[..., :B_NOPE], apply_rope(q_l[..., B_NOPE:], cos, sin)], axis=-1)
    kpe_l = apply_rope(kpe_l[:, :, None, :], cos, sin)[:, :, 0]
    k_l, v_l = keys_values(kv_a_l, kpe_l)
    k_c, v_c = keys_values(kv_a_c, kpe_c)

    k_all = jnp.concatenate([k_c, k_l], axis=1)
    v_all = jnp.concatenate([v_c, v_l], axis=1)

    def block(q_b):
        logits = jnp.einsum('bqhd,bkhd->bhqk', q_b, k_all).astype(jnp.float32) * scale
        p = jax.nn.softmax(logits, axis=-1)
        return jnp.einsum('bhqk,bkhd->bqhd', p.astype(v_all.dtype), v_all)

    nb = n_lat // B_BLOCK
    q_blocks = jnp.moveaxis(q_l.reshape(bsz, nb, B_BLOCK, B_HEADS, dqk), 1, 0)
    o = lax.map(block, q_blocks)
    o_l = jnp.moveaxis(o, 0, 1).reshape(bsz, n_lat, B_WIDTH)
    y_l = (o_l * jax.nn.silu(z_l)) @ w_out

    y_c = None
    if need_ctx:
        q_c = queries(q_a_c)
        logits = jnp.einsum('bqhd,bkhd->bhqk', q_c, k_c).astype(jnp.float32) * scale
        p = jax.nn.softmax(logits, axis=-1)
        o_c = jnp.einsum('bhqk,bkhd->bqhd', p.astype(v_c.dtype), v_c).reshape(bsz, n_ctx, B_WIDTH)
        y_c = (o_c * jax.nn.silu(z_c)) @ w_out
    return y_l, y_c


def mlstm_chunkwise(q, k, v, i_pre, f_pre, state, need_out):
    b, h, t, dk = q.shape
    dv = v.shape[-1]
    nc = t // C_CHUNK
    q = q.reshape(b, h, nc, C_CHUNK, dk)
    k = k.reshape(b, h, nc, C_CHUNK, dk)
    v = v.reshape(b, h, nc, C_CHUNK, dv)
    ig = i_pre.reshape(b, h, nc, C_CHUNK)
    cum = jnp.cumsum(jax.nn.log_sigmoid(f_pre).reshape(b, h, nc, C_CHUNK), axis=-1)
    tot = cum[..., -1]

    w_end = tot[..., None] - cum + ig
    m_loc = jnp.max(w_end, axis=-1)
    e_end = jnp.exp(w_end - m_loc[..., None])
    c_loc = jnp.einsum('bhcs,bhcsv,bhcsk->bhcvk', e_end, v, k)
    n_loc = jnp.einsum('bhcs,bhcsk->bhck', e_end, k)

    def step(carry, inp):
        c_prev, n_prev, m_prev = carry
        tot_c, m_l, c_l, n_l = inp
        m_new = jnp.maximum(tot_c + m_prev, m_l)
        a = jnp.exp(tot_c + m_prev - m_new)
        bb = jnp.exp(m_l - m_new)
        new = (a[..., None, None] * c_prev + bb[..., None, None] * c_l,
               a[..., None] * n_prev + bb[..., None] * n_l, m_new)
        return new, carry

    to_front = lambda a: jnp.moveaxis(a, 2, 0)
    final, starts = lax.scan(step, state, (to_front(tot), to_front(m_loc), to_front(c_loc), to_front(n_loc)))
    if not need_out:
        return None, final
    c_st, n_st, m_st = (jnp.moveaxis(s, 0, 2) for s in starts)

    tri = jnp.tril(jnp.ones((C_CHUNK, C_CHUNK), dtype=bool))
    log_d = jnp.where(tri, cum[..., :, None] - cum[..., None, :] + ig[..., None, :], NEG_INF)
    inter = cum + m_st[..., None]
    m_t = jnp.maximum(inter, jnp.max(log_d, axis=-1))
    s_qk = jnp.einsum('bhctd,bhcsd->bhcts', q, k) * jnp.exp(log_d - m_t[..., None])
    a_t = jnp.exp(inter - m_t)
    num = (jnp.einsum('bhcts,bhcsv->bhctv', s_qk, v)
           + a_t[..., None] * jnp.einsum('bhcvk,bhctk->bhctv', c_st, q))
    den = jnp.sum(s_qk, axis=-1) + a_t * jnp.einsum('bhck,bhctk->bhct', n_st, q)
    out = num / jnp.maximum(jnp.abs(den), jnp.exp(-m_t))[..., None]
    return out.reshape(b, h, t, dv), final


def mlstm_mixer(h_lat, h_ctx, need_ctx, w_in, conv, b_gate, g_head, w_out):
    bsz = h_lat.shape[0]
    cuts = [2 * C_QK_WIDTH, 2 * C_QK_WIDTH + C_WIDTH, 2 * C_QK_WIDTH + 2 * C_WIDTH, 2 * C_QK_WIDTH + 3 * C_WIDTH]

    def prepare(h):
        b, n, _ = h.shape
        qk, v, o, z, gates = jnp.split(h @ w_in, cuts, axis=-1)
        qk = jax.nn.silu(centred_dwconv(qk, conv))
        q, k = jnp.split(qk, 2, axis=-1)
        heads = lambda a, d: jnp.moveaxis(a.reshape(b, n, C_HEADS, d), 2, 1).astype(jnp.float32)
        q = heads(q, C_QK_DIM)
        k = heads(k, C_QK_DIM) * (C_QK_DIM ** -0.5)
        v = heads(v, C_V_DIM)
        gates = jnp.moveaxis((gates + b_gate).astype(jnp.float32).reshape(b, n, 4, C_HEADS), 1, -1)
        return q, k, v, o, z, gates

    def finish(hsum, o, z):
        b, n, _ = o.shape
        hh = jnp.moveaxis(hsum, 1, 2).astype(o.dtype)
        hh = hh * jax.nn.sigmoid(o).reshape(b, n, C_HEADS, C_V_DIM)
        hh = rmsnorm(hh, g_head.reshape(C_HEADS, C_V_DIM)).reshape(b, n, C_WIDTH)
        return (hh * jax.nn.silu(z)) @ w_out

    q_c, k_c, v_c, o_c, z_c, g_c = prepare(h_ctx)
    q_l, k_l, v_l, o_l, z_l, g_l = prepare(h_lat)
    zero_state = (jnp.zeros((bsz, C_HEADS, C_V_DIM, C_QK_DIM), jnp.float32),
                  jnp.zeros((bsz, C_HEADS, C_QK_DIM), jnp.float32),
                  jnp.full((bsz, C_HEADS), NEG_INF, jnp.float32))
    flip = lambda a: jnp.flip(a, axis=2)

    hc_f, st_f = mlstm_chunkwise(q_c, k_c, v_c, g_c[:, 0], g_c[:, 1], zero_state, need_ctx)
    hl_f, _ = mlstm_chunkwise(q_l, k_l, v_l, g_l[:, 0], g_l[:, 1], st_f, True)
    hc_b, st_b = mlstm_chunkwise(flip(q_c), flip(k_c), flip(v_c), flip(g_c[:, 2]), flip(g_c[:, 3]), zero_state, need_ctx)
    hl_b, _ = mlstm_chunkwise(flip(q_l), flip(k_l), flip(v_l), flip(g_l[:, 2]), flip(g_l[:, 3]), st_b, True)

    y_l = finish(hl_f + flip(hl_b), o_l, z_l)
    y_c = finish(hc_f + flip(hc_b), o_c, z_c) if need_ctx else None
    return y_l, y_c


def _normal(key, shape, std):
    return std * jax.random.normal(key, shape, jnp.float32)


def _gain(key, n):
    return 1.0 + 0.1 * jax.random.normal(key, (n,), jnp.float32)


def _layer_params(key, kind):
    ks = jax.random.split(key, 10)
    p = {
        'w_ada': _normal(ks[0], (D_MODEL, 3 * D_MODEL), ADA_STD),
        'b_ada': _normal(ks[1], (3 * D_MODEL,), 0.02),
        'g_pre': _gain(ks[2], D_MODEL),
        'g_post': _gain(ks[3], D_MODEL),
    }
    if kind == 0:
        p['w_in'] = _normal(ks[4], (D_MODEL, A_IN), D_MODEL ** -0.5)
        p['sink'] = _normal(ks[5], (A_HEADS,), 0.5)
        p['w_out'] = _normal(ks[6], (A_WIDTH, D_MODEL), A_WIDTH ** -0.5)
    elif kind == 1:
        p['w_in'] = _normal(ks[4], (D_MODEL, B_IN), D_MODEL ** -0.5)
        p['g_qa'] = _gain(ks[5], B_Q_RANK)
        p['g_kva'] = _gain(ks[6], B_KV_RANK)
        p['w_uq'] = _normal(ks[7], (B_Q_RANK, B_HEADS * (B_NOPE + B_ROPE)), B_Q_RANK ** -0.5)
        p['w_ukv'] = _normal(ks[8], (B_KV_RANK, B_HEADS * (B_NOPE + B_VDIM)), B_KV_RANK ** -0.5)
        p['w_out'] = _normal(ks[9], (B_WIDTH, D_MODEL), B_WIDTH ** -0.5)
    else:
        offset = jnp.repeat(jnp.array([0.0, F_BIAS, 0.0, F_BIAS], jnp.float32), C_HEADS)
        p['w_in'] = _normal(ks[4], (D_MODEL, C_IN), D_MODEL ** -0.5)
        p['conv'] = _normal(ks[5], (C_CONV, 2 * C_QK_WIDTH), C_CONV ** -0.5)
        p['b_gate'] = offset + _normal(ks[6], (4 * C_HEADS,), 0.3)
        p['g_head'] = _gain(ks[7], C_WIDTH)
        p['w_out'] = _normal(ks[8], (C_WIDTH, D_MODEL), C_WIDTH ** -0.5)
    return p


def setup_inputs(seed: int = 0) -> dict:
    key = jax.random.key(seed)
    k_x, k_c, k_ctx, k_cc, k_layers = jax.random.split(key, 5)
    inputs = {
        'x': jax.random.normal(k_x, (BATCH, SEQ, D_MODEL), jnp.float32),
        'c': jax.random.normal(k_c, (BATCH, D_MODEL), jnp.float32),
        'ctx': jax.random.normal(k_ctx, (BATCH, CTX_LEN, D_MODEL), jnp.float32),
        'c_ctx': jax.random.normal(k_cc, (D_MODEL,), jnp.float32),
    }
    for i, lk in enumerate(jax.random.split(k_layers, DEPTH)):
        for name, val in _layer_params(lk, i % N_MIXERS).items():
            inputs[f'l{i}_{name}'] = val
    return inputs


def modulation(cvec, w_ada, b_ada):
    return jnp.split(jax.nn.silu(cvec) @ w_ada + b_ada, 3, axis=-1)


def reference(x, c, ctx, c_ctx,
              l0_w_ada, l0_b_ada, l0_g_pre, l0_g_post, l0_w_in, l0_sink, l0_w_out,
              l1_w_ada, l1_b_ada, l1_g_pre, l1_g_post, l1_w_in, l1_g_qa, l1_g_kva, l1_w_uq, l1_w_ukv, l1_w_out,
              l2_w_ada, l2_b_ada, l2_g_pre, l2_g_post, l2_w_in, l2_conv, l2_b_gate, l2_g_head, l2_w_out,
              l3_w_ada, l3_b_ada, l3_g_pre, l3_g_post, l3_w_in, l3_sink, l3_w_out):
    layers = [
        (l0_w_ada, l0_b_ada, l0_g_pre, l0_g_post,
         functools.partial(window_gqa_mixer, w_in=l0_w_in, sink=l0_sink, w_out=l0_w_out)),
        (l1_w_ada, l1_b_ada, l1_g_pre, l1_g_post,
         functools.partial(mla_mixer, w_in=l1_w_in, g_qa=l1_g_qa, g_kva=l1_g_kva,
                           w_uq=l1_w_uq, w_ukv=l1_w_ukv, w_out=l1_w_out)),
        (l2_w_ada, l2_b_ada, l2_g_pre, l2_g_post,
         functools.partial(mlstm_mixer, w_in=l2_w_in, conv=l2_conv, b_gate=l2_b_gate,
                           g_head=l2_g_head, w_out=l2_w_out)),
        (l3_w_ada, l3_b_ada, l3_g_pre, l3_g_post,
         functools.partial(window_gqa_mixer, w_in=l3_w_in, sink=l3_sink, w_out=l3_w_out)),
    ]
    x_lat, x_ctx = x, ctx
    for i in range(DEPTH):
        w_ada, b_ada, g_pre, g_post, mixer = layers[i]
        need_ctx = i < DEPTH - 1
        sh_l, sc_l, gt_l = modulation(c, w_ada, b_ada)
        sh_c, sc_c, gt_c = modulation(c_ctx, w_ada, b_ada)
        h_l = rmsnorm(x_lat, g_pre) * (1.0 + sc_l[:, None, :]) + sh_l[:, None, :]
        h_c = rmsnorm(x_ctx, g_pre) * (1.0 + sc_c) + sh_c
        y_l, y_c = mixer(h_l, h_c, need_ctx=need_ctx)
        x_lat = x_lat + gt_l[:, None, :] * rmsnorm(y_l, g_post)
        if need_ctx:
            x_ctx = x_ctx + gt_c * rmsnorm(y_c, g_post)
    return x_lat
```

```python
import functools

import jax
import jax.numpy as jnp
from jax import lax
from jax.experimental import pallas as pl
from jax.experimental.pallas import tpu as pltpu

F32 = jnp.float32
BF16 = jnp.bfloat16

GRID_W = 64
EPS = 1e-6
NEG_INF = -1e30
ROPE_BASE = 10000.0
A_HEADS, A_KV_HEADS, A_HEAD_DIM = 16, 4, 64
A_WIDTH, A_KV_WIDTH = A_HEADS * A_HEAD_DIM, A_KV_HEADS * A_HEAD_DIM
B_HEADS, B_NOPE, B_ROPE, B_VDIM = 8, 128, 64, 128
B_Q_RANK, B_KV_RANK = 256, 128
B_WIDTH = B_HEADS * B_VDIM
C_HEADS, C_QK_DIM, C_V_DIM, C_CONV = 4, 128, 256, 5
C_QK_WIDTH, C_WIDTH = C_HEADS * C_QK_DIM, C_HEADS * C_V_DIM

LANES = 128
SUBLANES = 8
ROW_TILE = 256
ATT_BLOCK = 128
MLA_KEY_TILE = 512
VMEM_LIMIT = 56 * 1024 * 1024

A_HEAD_ORDER = tuple(
    (2 * g + half) * (A_HEADS // A_KV_HEADS) + i
    for g in range(A_KV_HEADS // 2) for i in range(A_HEADS // A_KV_HEADS) for half in range(2))


def _cparams(sem):
    return pltpu.CompilerParams(dimension_semantics=sem, vmem_limit_bytes=VMEM_LIMIT)


def _silu(x):
    return x * jax.nn.sigmoid(x)


def _rmsnorm_rows(x, g):
    return x * lax.rsqrt(jnp.mean(x * x, axis=-1, keepdims=True) + EPS) * g


def _nt_dot(a, b):
    return lax.dot_general(a, b, (((1,), (1,)), ((), ())), preferred_element_type=F32)


def _dot(a, b):
    return jnp.dot(a, b, preferred_element_type=F32)


def _mod_kernel(c_ref, w_ref, b_ref, o_ref):
    s = _silu(c_ref[...])
    o_ref[...] = jnp.dot(s, w_ref[...], preferred_element_type=F32,
                         precision=lax.Precision.HIGHEST) + b_ref[...]


def _modulation(cvec, w_ada, b_ada):
    d = cvec.shape[1]
    n = w_ada.shape[1]
    return pl.pallas_call(
        _mod_kernel,
        grid=(n // d,),
        in_specs=[pl.BlockSpec((SUBLANES, d), lambda j: (0, 0)),
                  pl.BlockSpec((d, d), lambda j: (0, j)),
                  pl.BlockSpec((1, d), lambda j: (0, j))],
        out_specs=pl.BlockSpec((SUBLANES, d), lambda j: (0, j)),
        out_shape=jax.ShapeDtypeStruct((SUBLANES, n), F32),
        compiler_params=_cparams(("arbitrary",)),
        name="modulation",
    )(cvec, w_ada, b_ada.reshape(1, n))


def _prenorm(x_ref, g_ref, sc_ref, sh_ref):
    h = _rmsnorm_rows(x_ref[0], g_ref[...]) * (1.0 + sc_ref[0]) + sh_ref[0]
    return h.astype(BF16)


def _rope_group(xg, cos, sin_signed, low_half):
    swapped = jnp.where(low_half, pltpu.roll(xg, LANES - 32, 1), pltpu.roll(xg, 32, 1))
    return xg * cos + swapped * sin_signed


def _low_half_mask(rows):
    lane = lax.broadcasted_iota(jnp.int32, (rows, LANES), 1)
    return (lane & 63) < 32


def _row_specs(d, n_batch):
    mod_map = lambda b, t: (jnp.where(t == 0, n_batch, b), 0, 0)
    return [pl.BlockSpec((1, ROW_TILE, d), lambda b, t: (b, t, 0)),
            pl.BlockSpec((1, d), lambda b, t: (0, 0)),
            pl.BlockSpec((1, 1, d), mod_map),
            pl.BlockSpec((1, 1, d), mod_map)]


def _whole(shape):
    return pl.BlockSpec(shape, lambda b, t: (0,) * len(shape))


def _rows(width):
    return pl.BlockSpec((1, ROW_TILE, width), lambda b, t: (b, t, 0))


def _in_a_kernel(x_ref, g_ref, sc_ref, sh_ref, w_ref, cos_ref, sin_ref,
                 q_ref, k_ref, v_ref, z_ref):
    hb = _prenorm(x_ref, g_ref, sc_ref, sh_ref)
    cos, sin = cos_ref[...], sin_ref[...]
    low = _low_half_mask(ROW_TILE)
    scale = A_HEAD_DIM ** -0.5
    q = _dot(hb, w_ref[:, :A_WIDTH])
    for j in range(A_WIDTH // LANES):
        sl = slice(j * LANES, (j + 1) * LANES)
        q_ref[0, :, sl] = (_rope_group(q[:, sl], cos, sin, low) * scale).astype(BF16)
    kv = _dot(hb, w_ref[:, A_WIDTH:A_WIDTH + 2 * A_KV_WIDTH])
    for j in range(A_KV_WIDTH // LANES):
        sl = slice(j * LANES, (j + 1) * LANES)
        k_ref[0, :, sl] = _rope_group(kv[:, sl], cos, sin, low).astype(BF16)
    v_ref[0] = kv[:, A_KV_WIDTH:].astype(BF16)
    z = _dot(hb, w_ref[:, A_WIDTH + 2 * A_KV_WIDTH:])
    z_ref[0] = _silu(z).astype(BF16)


def _in_proj_a(x_all, g_pre, scale, shift, w, cos_t, sin_t, n_batch):
    bsz, rows, d = x_all.shape
    grid = (bsz, rows // ROW_TILE)
    tab = pl.BlockSpec((ROW_TILE, LANES), lambda b, t: (t, 0))
    out = lambda width: jax.ShapeDtypeStruct((bsz, rows, width), BF16)
    return pl.pallas_call(
        _in_a_kernel,
        grid=grid,
        in_specs=_row_specs(d, n_batch) + [_whole(w.shape), tab, tab],
        out_specs=[_rows(A_WIDTH), _rows(A_KV_WIDTH), _rows(A_KV_WIDTH), _rows(A_WIDTH)],
        out_shape=[out(A_WIDTH), out(A_KV_WIDTH), out(A_KV_WIDTH), out(A_WIDTH)],
        compiler_params=_cparams(("parallel", "parallel")),
        name="in_proj_gqa",
    )(x_all, g_pre, scale, shift, w, cos_t, sin_t)


def _attn_a_kernel(sink_ref, q_ref, kp_ref, kc_ref, kn_ref, kx_ref,
                   vp_ref, vc_ref, vn_ref, vx_ref, gz_ref, o_ref, *, n_blocks, ctx_blocks):
    blk = pl.program_id(1)
    is_lat = blk >= ctx_blocks
    far = 4 * ATT_BLOCK
    off_prev = jnp.where(blk > ctx_blocks, 0, far)
    off_cur = jnp.where(is_lat, 0, far)
    off_next = jnp.where(jnp.logical_and(is_lat, blk < n_blocks - 1), 0, far)
    row = lax.broadcasted_iota(jnp.int32, (ATT_BLOCK, ATT_BLOCK), 0)
    col = lax.broadcasted_iota(jnp.int32, (ATT_BLOCK, ATT_BLOCK), 1)
    n_ctx = ctx_blocks * ATT_BLOCK
    mask = jnp.concatenate(
        [col >= row + off_prev, col >= off_cur, col <= row - off_next,
         jnp.ones((ATT_BLOCK, n_ctx), jnp.bool_)], axis=1)[None]
    lane = lax.broadcasted_iota(jnp.int32, (ATT_BLOCK, LANES), 1)
    lane_lo = lane < A_HEAD_DIM
    keys = jnp.concatenate([kp_ref[0], kc_ref[0], kn_ref[0], kx_ref[0]], axis=0)
    vals = jnp.concatenate([vp_ref[0], vc_ref[0], vn_ref[0], vx_ref[0]], axis=0)
    n_keys = keys.shape[0]
    grp = A_HEADS // A_KV_HEADS
    zero = jnp.zeros((), BF16)
    for g in range(A_KV_HEADS // 2):
        kg = keys[:, g * LANES:(g + 1) * LANES]
        vg = vals[:, g * LANES:(g + 1) * LANES]
        parts = []
        for i in range(grp):
            qv = q_ref[0, :, (g * grp + i) * LANES:(g * grp + i + 1) * LANES]
            parts += [jnp.where(lane_lo, qv, zero), jnp.where(lane_lo, zero, qv)]
        qs = jnp.concatenate(parts, axis=0)
        logits = _nt_dot(qs, kg).reshape(2 * grp, ATT_BLOCK, n_keys)
        logits = jnp.where(mask, logits, NEG_INF)
        sink = jnp.concatenate(
            [jnp.full((1, 1, 1), sink_ref[A_HEAD_ORDER[2 * (g * grp + i) + half]], F32)
             for i in range(grp) for half in range(2)], axis=0)
        m = jnp.maximum(jnp.max(logits, axis=-1, keepdims=True), sink)
        e = jnp.exp(logits - m)
        den = jnp.sum(e, axis=-1, keepdims=True) + jnp.exp(sink - m)
        pv = _dot(e.astype(BF16).reshape(2 * grp * ATT_BLOCK, n_keys), vg)
        pv = pv.reshape(2 * grp, ATT_BLOCK, LANES) / den
        for i in range(grp):
            sl = slice((g * grp + i) * LANES, (g * grp + i + 1) * LANES)
            o = jnp.where(lane_lo, pv[2 * i], pv[2 * i + 1])
            o_ref[0, :, sl] = (o * gz_ref[0, :, sl].astype(F32)).astype(BF16)


def _attention_a(sink, q, k, v, gz, n_ctx):
    bsz, rows, _ = q.shape
    n_blocks = rows // ATT_BLOCK
    ctx_blocks = n_ctx // ATT_BLOCK
    lo, hi = ctx_blocks, n_blocks - 1
    kv_block = lambda f: pl.BlockSpec((1, ATT_BLOCK, A_KV_WIDTH), lambda b, t: (b, f(t), 0))
    prev = kv_block(lambda t: jnp.clip(t - 1, lo, hi))
    cur = kv_block(lambda t: jnp.clip(t, lo, hi))
    nxt = kv_block(lambda t: jnp.clip(t + 1, lo, hi))
    ctx = pl.BlockSpec((1, n_ctx, A_KV_WIDTH), lambda b, t: (b, 0, 0))
    qspec = pl.BlockSpec((1, ATT_BLOCK, A_WIDTH), lambda b, t: (b, t, 0))
    return pl.pallas_call(
        functools.partial(_attn_a_kernel, n_blocks=n_blocks, ctx_blocks=ctx_blocks),
        grid=(bsz, n_blocks),
        in_specs=[pl.BlockSpec(memory_space=pltpu.SMEM), qspec,
                  prev, cur, nxt, ctx, prev, cur, nxt, ctx, qspec],
        out_specs=qspec,
        out_shape=jax.ShapeDtypeStruct((bsz, rows, A_WIDTH), BF16),
        compiler_params=_cparams(("parallel", "parallel")),
        name="attention_gqa",
    )(sink, q, k, k, k, k, v, v, v, v, gz)


def _out_kernel(a_ref, w_ref, x_ref, gate_ref, g_ref, o_ref):
    y = _dot(a_ref[0], w_ref[...])
    o_ref[0] = x_ref[0] + gate_ref[0] * _rmsnorm_rows(y, g_ref[...])


def _out_proj(act, w_out, x_all, gate, g_post, n_batch, latent_only):
    bsz, rows, d = x_all.shape
    tiles = rows // ROW_TILE
    skip = 1 if latent_only else 0
    mod_map = lambda b, t: (jnp.where(t + skip == 0, n_batch, b), 0, 0)
    return pl.pallas_call(
        _out_kernel,
        grid=(bsz, tiles - skip),
        in_specs=[pl.BlockSpec((1, ROW_TILE, act.shape[2]), lambda b, t: (b, t + skip, 0)),
                  _whole(w_out.shape),
                  pl.BlockSpec((1, ROW_TILE, d), lambda b, t: (b, t + skip, 0)),
                  pl.BlockSpec((1, 1, d), mod_map),
                  pl.BlockSpec((1, d), lambda b, t: (0, 0))],
        out_specs=pl.BlockSpec((1, ROW_TILE, d), lambda b, t: (b, t, 0)),
        out_shape=jax.ShapeDtypeStruct((bsz, rows - skip * ROW_TILE, d), F32),
        compiler_params=_cparams(("parallel", "parallel")),
        name="out_proj",
    )(act, w_out, x_all, gate, g_post)


def _absorb_kernel(wq_ref, wk_ref, o_ref):
    o_ref[0] = lax.dot_general(wq_ref[0], wk_ref[0], (((1,), (1,)), ((), ())),
                               preferred_element_type=F32, precision=lax.Precision.HIGHEST)


def _absorb_weights(wq_nope, wk_nope):
    heads, q_rank, nope = wq_nope.shape
    kv_rank = wk_nope.shape[1]
    return pl.pallas_call(
        _absorb_kernel,
        grid=(heads,),
        in_specs=[pl.BlockSpec((1, q_rank, nope), lambda h: (h, 0, 0)),
                  pl.BlockSpec((1, kv_rank, nope), lambda h: (h, 0, 0))],
        out_specs=pl.BlockSpec((1, q_rank, kv_rank), lambda h: (h, 0, 0)),
        out_shape=jax.ShapeDtypeStruct((heads, q_rank, kv_rank), F32),
        compiler_params=_cparams(("arbitrary",)),
        name="mla_absorb",
    )(wq_nope, wk_nope)


def _in_b_kernel(x_ref, g_ref, sc_ref, sh_ref, w_ref, gq_ref, gk_ref, wq_ref, cos_ref, sin_ref,
                 q_ref, k_ref, z_ref):
    hb = _prenorm(x_ref, g_ref, sc_ref, sh_ref)
    cos, sin = cos_ref[...], sin_ref[...]
    low = _low_half_mask(ROW_TILE)
    lane = lax.broadcasted_iota(jnp.int32, (ROW_TILE, LANES), 1)
    n_a = B_Q_RANK + B_KV_RANK + LANES
    a = _dot(hb, w_ref[:, :n_a])
    z_ref[0] = _silu(_dot(hb, w_ref[:, n_a:])).astype(BF16)
    k_ref[0, :, :LANES] = _rmsnorm_rows(a[:, B_Q_RANK:B_Q_RANK + B_KV_RANK], gk_ref[...]).astype(BF16)
    k_ref[0, :, LANES:] = _rope_group(a[:, B_Q_RANK + B_KV_RANK:], cos, sin, low).astype(BF16)
    qn = _rmsnorm_rows(a[:, :B_Q_RANK], gq_ref[...]).astype(BF16)
    qq = _dot(qn, wq_ref[...]) * ((B_NOPE + B_ROPE) ** -0.5)
    nope_w = B_HEADS * B_KV_RANK
    for pair in range(B_HEADS // 2):
        rp = _rope_group(qq[:, nope_w + pair * LANES:nope_w + (pair + 1) * LANES], cos, sin, low)
        for half in range(2):
            h = 2 * pair + half
            q_ref[0, h, :, :LANES] = qq[:, h * B_KV_RANK:(h + 1) * B_KV_RANK].astype(BF16)
            own = (lane < B_ROPE) if half == 0 else (lane >= B_ROPE)
            q_ref[0, h, :, LANES:] = jnp.where(own, rp, 0.0).astype(BF16)


def _in_proj_b(x_all, g_pre, scale, shift, w, g_qa, g_kva, wq, cos_t, sin_t, n_batch):
    bsz, rows, d = x_all.shape
    tab = pl.BlockSpec((ROW_TILE, LANES), lambda b, t: (t, 0))
    return pl.pallas_call(
        _in_b_kernel,
        grid=(bsz, rows // ROW_TILE),
        in_specs=_row_specs(d, n_batch) + [_whole(w.shape), _whole(g_qa.shape), _whole(g_kva.shape),
                                           _whole(wq.shape), tab, tab],
        out_specs=[pl.BlockSpec((1, B_HEADS, ROW_TILE, 2 * LANES), lambda b, t: (b, 0, t, 0)),
                   _rows(2 * LANES), _rows(B_WIDTH)],
        out_shape=[jax.ShapeDtypeStruct((bsz, B_HEADS, rows, 2 * LANES), BF16),
                   jax.ShapeDtypeStruct((bsz, rows, 2 * LANES), BF16),
                   jax.ShapeDtypeStruct((bsz, rows, B_WIDTH), BF16)],
        compiler_params=_cparams(("parallel", "parallel")),
        name="in_proj_mla",
    )(x_all, g_pre, scale, shift, w, g_qa, g_kva, wq, cos_t, sin_t)


def _attn_b_kernel(q_ref, k_ref, wv_ref, gz_ref, o_ref, m_ref, l_ref, acc_ref, *, n_ctx, n_lat):
    tile = pl.program_id(1)
    qs = q_ref[0].reshape(B_HEADS * ROW_TILE, 2 * LANES)

    def scores(kt):
        return _nt_dot(qs, kt)

    k0 = k_ref[0, :n_ctx, :]
    s = scores(k0)
    m0 = jnp.max(s, axis=-1, keepdims=True)
    e = jnp.exp(s - m0)
    m_ref[...] = m0
    l_ref[...] = jnp.sum(e, axis=-1, keepdims=True)
    acc_ref[...] = _dot(e.astype(BF16), k0[:, :B_KV_RANK])

    def body(j, carry):
        start = pl.multiple_of(n_ctx + j * MLA_KEY_TILE, MLA_KEY_TILE // 2)
        kt = k_ref[0, pl.ds(start, MLA_KEY_TILE), :]
        s = scores(kt)
        m_old = m_ref[...]
        m_new = jnp.maximum(m_old, jnp.max(s, axis=-1, keepdims=True))
        alpha = jnp.exp(m_old - m_new)
        e = jnp.exp(s - m_new)
        m_ref[...] = m_new
        l_ref[...] = alpha * l_ref[...] + jnp.sum(e, axis=-1, keepdims=True)
        acc_ref[...] = alpha * acc_ref[...] + _dot(e.astype(BF16), kt[:, :B_KV_RANK])
        return carry

    lax.fori_loop(0, jnp.where(tile == 0, 0, n_lat // MLA_KEY_TILE), body, 0)

    o = (acc_ref[...] / l_ref[...]).astype(BF16)
    for h in range(B_HEADS):
        oh = _dot(o[h * ROW_TILE:(h + 1) * ROW_TILE], wv_ref[h])
        sl = slice(h * B_VDIM, (h + 1) * B_VDIM)
        o_ref[0, :, sl] = (oh * gz_ref[0, :, sl].astype(F32)).astype(BF16)


def _attention_b(q, k, wv, gz, n_ctx):
    bsz, heads, rows, qw = q.shape
    m_rows = heads * ROW_TILE
    return pl.pallas_call(
        functools.partial(_attn_b_kernel, n_ctx=n_ctx, n_lat=rows - n_ctx),
        grid=(bsz, rows // ROW_TILE),
        in_specs=[pl.BlockSpec((1, heads, ROW_TILE, qw), lambda b, t: (b, 0, t, 0)),
                  pl.BlockSpec((1, rows, qw), lambda b, t: (b, 0, 0)),
                  _whole(wv.shape), _rows(B_WIDTH)],
        out_specs=_rows(B_WIDTH),
        out_shape=jax.ShapeDtypeStruct((bsz, rows, B_WIDTH), BF16),
        scratch_shapes=[pltpu.VMEM((m_rows, 1), F32), pltpu.VMEM((m_rows, 1), F32),
                        pltpu.VMEM((m_rows, B_KV_RANK), F32)],
        compiler_params=_cparams(("parallel", "arbitrary")),
        name="attention_mla",
    )(q, k, wv, gz)


def _in_c_kernel(x_ref, g_ref, sc_ref, sh_ref, w_ref, bg_ref, qk_ref, v_ref, o_ref, z_ref, gt_ref):
    hb = _prenorm(x_ref, g_ref, sc_ref, sh_ref)
    qk_ref[0] = _dot(hb, w_ref[:, :2 * C_QK_WIDTH])
    c0 = 2 * C_QK_WIDTH
    v_ref[0] = _dot(hb, w_ref[:, c0:c0 + C_WIDTH]).astype(BF16)
    o_ref[0] = jax.nn.sigmoid(_dot(hb, w_ref[:, c0 + C_WIDTH:c0 + 2 * C_WIDTH])).astype(BF16)
    z_ref[0] = _silu(_dot(hb, w_ref[:, c0 + 2 * C_WIDTH:c0 + 3 * C_WIDTH])).astype(BF16)
    gates = _dot(hb, w_ref[:, c0 + 3 * C_WIDTH:])[:, :4 * C_HEADS] + bg_ref[...]
    col = lax.broadcasted_iota(jnp.int32, gates.shape, 1)
    is_forget = ((col >> 2) & 1) == 1
    log_sig = jnp.minimum(gates, 0.0) - jnp.log1p(jnp.exp(-jnp.abs(gates)))
    gt_ref[0] = jnp.where(is_forget, log_sig, gates)


def _in_proj_c(x_all, g_pre, scale, shift, w, b_gate, n_batch):
    bsz, rows, d = x_all.shape
    out = lambda width, dt: jax.ShapeDtypeStruct((bsz, rows, width), dt)
    ng = 4 * C_HEADS
    return pl.pallas_call(
        _in_c_kernel,
        grid=(bsz, rows // ROW_TILE),
        in_specs=_row_specs(d, n_batch) + [_whole(w.shape), _whole(b_gate.shape)],
        out_specs=[_rows(2 * C_QK_WIDTH), _rows(C_WIDTH), _rows(C_WIDTH), _rows(C_WIDTH), _rows(ng)],
        out_shape=[out(2 * C_QK_WIDTH, F32), out(C_WIDTH, BF16), out(C_WIDTH, BF16),
                   out(C_WIDTH, BF16), out(ng, F32)],
        compiler_params=_cparams(("parallel", "parallel")),
        name="in_proj_mlstm",
    )(x_all, g_pre, scale, shift, w, b_gate)


def _conv_kernel(cur_ref, prev_ref, next_ref, w_ref, o_ref, buf_ref, *, n_tiles):
    tile = pl.program_id(1)
    pad = SUBLANES
    half = C_CONV // 2
    has_prev = (tile >= 2).astype(F32)
    has_next = jnp.logical_and(tile >= 1, tile < n_tiles - 1).astype(F32)
    buf_ref[:pad] = prev_ref[0] * has_prev
    buf_ref[pad:pad + ROW_TILE] = cur_ref[0]
    buf_ref[pad + ROW_TILE:] = next_ref[0] * has_next
    acc = None
    for j in range(C_CONV):
        term = buf_ref[pl.ds(pad - half + j, ROW_TILE), :] * w_ref[j:j + 1, :]
        acc = term if acc is None else acc + term
    act = _silu(acc)
    o_ref[0, :, :C_QK_WIDTH] = act[:, :C_QK_WIDTH].astype(BF16)
    o_ref[0, :, C_QK_WIDTH:] = (act[:, C_QK_WIDTH:] * (C_QK_DIM ** -0.5)).astype(BF16)


def _conv_qk(qk_raw, conv_w):
    bsz, rows, width = qk_raw.shape
    n_tiles = rows // ROW_TILE
    per = ROW_TILE // SUBLANES
    last = rows // SUBLANES - 1
    return pl.pallas_call(
        functools.partial(_conv_kernel, n_tiles=n_tiles),
        grid=(bsz, n_tiles),
        in_specs=[_rows(width),
                  pl.BlockSpec((1, SUBLANES, width), lambda b, t: (b, jnp.maximum(t * per - 1, 0), 0)),
                  pl.BlockSpec((1, SUBLANES, width), lambda b, t: (b, jnp.minimum((t + 1) * per, last), 0)),
                  _whole(conv_w.shape)],
        out_specs=_rows(width),
        out_shape=jax.ShapeDtypeStruct((bsz, rows, width), BF16),
        scratch_shapes=[pltpu.VMEM((ROW_TILE + 2 * SUBLANES, width), F32)],
        compiler_params=_cparams(("parallel", "parallel")),
        name="conv_qk",
    )(qk_raw, qk_raw, qk_raw, conv_w)


def _mlstm_chunk(q, k, v, ig_col, lf_col, ig_row, lf_row, ct_ref, n_ref, m_ref, head, reverse):
    L = ATT_BLOCK
    t_idx = lax.broadcasted_iota(jnp.int32, (L, L), 0)
    s_idx = lax.broadcasted_iota(jnp.int32, (L, L), 1)
    seen = (s_idx >= t_idx) if reverse else (s_idx <= t_idx)
    cum_col = jnp.sum(jnp.where(seen, lf_row, 0.0), axis=1, keepdims=True)
    cum_row = jnp.sum(jnp.where(seen, 0.0, lf_col) + jnp.where(s_idx == t_idx, lf_col, 0.0),
                      axis=0, keepdims=True)
    tot = jnp.sum(lf_row, axis=1, keepdims=True)
    m_st = m_ref[head]
    ct = ct_ref[head]
    n_st = n_ref[head]

    log_d = jnp.where(seen, cum_col - cum_row + ig_row, NEG_INF)
    inter = cum_col + m_st
    m_t = jnp.maximum(inter, jnp.max(log_d, axis=1, keepdims=True))
    s_qk = _nt_dot(q, k) * jnp.exp(log_d - m_t)
    a_t = jnp.exp(inter - m_t)
    qf = q.astype(F32)
    num = _dot(s_qk.astype(BF16), v) + a_t * _dot(q, ct.astype(BF16))
    den = jnp.sum(s_qk, axis=1, keepdims=True) + a_t * jnp.sum(qf * n_st, axis=1, keepdims=True)
    out = num / jnp.maximum(jnp.abs(den), jnp.exp(-m_t))

    w_end = tot - cum_col + ig_col
    m_loc = jnp.max(w_end, axis=0, keepdims=True)
    e_end = jnp.exp(w_end - m_loc)
    kf = k.astype(F32)
    c_loc = _dot(kf.T.astype(BF16), (v.astype(F32) * e_end).astype(BF16))
    n_loc = jnp.sum(kf * e_end, axis=0, keepdims=True)
    m_new = jnp.maximum(tot + m_st, m_loc)
    a = jnp.exp(tot + m_st - m_new)
    bb = jnp.exp(m_loc - m_new)
    ct_ref[head] = a * ct + bb * c_loc
    n_ref[head] = a * n_st + bb * n_loc
    m_ref[head] = m_new
    return out


def _mlstm_kernel(qkf_ref, vf_ref, gcf_ref, grf_ref, qkb_ref, vb_ref, gcb_ref, grb_ref,
                  hf_ref, hb_ref, ctf_ref, nf_ref, mf_ref, ctb_ref, nb_ref, mb_ref):
    @pl.when(pl.program_id(1) == 0)
    def _():
        for ct, n, m in ((ctf_ref, nf_ref, mf_ref), (ctb_ref, nb_ref, mb_ref)):
            ct[...] = jnp.zeros(ct.shape, F32)
            n[...] = jnp.zeros(n.shape, F32)
            m[...] = jnp.full(m.shape, NEG_INF, F32)

    dirs = ((qkf_ref, vf_ref, gcf_ref, grf_ref, hf_ref, ctf_ref, nf_ref, mf_ref, False),
            (qkb_ref, vb_ref, gcb_ref, grb_ref, hb_ref, ctb_ref, nb_ref, mb_ref, True))
    for d, (qk, v, gc, gr, out, ct, n, m, reverse) in enumerate(dirs):
        for h in range(C_HEADS):
            i_idx = (2 * d) * C_HEADS + h
            f_idx = (2 * d + 1) * C_HEADS + h
            out[0, :, h * C_V_DIM:(h + 1) * C_V_DIM] = _mlstm_chunk(
                qk[0, :, h * C_QK_DIM:(h + 1) * C_QK_DIM],
                qk[0, :, C_QK_WIDTH + h * C_QK_DIM:C_QK_WIDTH + (h + 1) * C_QK_DIM],
                v[0, :, h * C_V_DIM:(h + 1) * C_V_DIM],
                gc[0, :, i_idx:i_idx + 1], gc[0, :, f_idx:f_idx + 1],
                gr[0, i_idx:i_idx + 1, :], gr[0, f_idx:f_idx + 1, :],
                ct, n, m, h, reverse)


def _mlstm(qk, v, gates, n_ctx):
    bsz, rows, _ = qk.shape
    n_chunks = rows // ATT_BLOCK
    ctx_chunks = n_ctx // ATT_BLOCK
    gates_t = jnp.swapaxes(gates, 1, 2)
    ng = gates.shape[2]
    fwd = lambda g: g
    bwd = lambda g: jnp.where(g < ctx_chunks, ctx_chunks - 1 - g, n_chunks - 1 + ctx_chunks - g)

    def specs(order):
        return [pl.BlockSpec((1, ATT_BLOCK, 2 * C_QK_WIDTH), lambda b, g: (b, order(g), 0)),
                pl.BlockSpec((1, ATT_BLOCK, C_WIDTH), lambda b, g: (b, order(g), 0)),
                pl.BlockSpec((1, ATT_BLOCK, ng), lambda b, g: (b, order(g), 0)),
                pl.BlockSpec((1, ng, ATT_BLOCK), lambda b, g: (b, 0, order(g)))]

    hspec = lambda order: pl.BlockSpec((1, ATT_BLOCK, C_WIDTH), lambda b, g: (b, order(g), 0))
    state = [pltpu.VMEM((C_HEADS, C_QK_DIM, C_V_DIM), F32), pltpu.VMEM((C_HEADS, 1, C_QK_DIM), F32),
             pltpu.VMEM((C_HEADS, 1, 1), F32)]
    return pl.pallas_call(
        _mlstm_kernel,
        grid=(bsz, n_chunks),
        in_specs=specs(fwd) + specs(bwd),
        out_specs=[hspec(fwd), hspec(bwd)],
        out_shape=[jax.ShapeDtypeStruct((bsz, rows, C_WIDTH), F32)] * 2,
        scratch_shapes=state + state,
        compiler_params=_cparams(("arbitrary", "arbitrary")),
        name="mlstm_scan",
    )(qk, v, gates, gates_t, qk, v, gates, gates_t)


def _out_c_kernel(hf_ref, hb_ref, og_ref, gz_ref, gh_ref, w_ref, x_ref, gate_ref, g_ref, o_ref):
    parts = []
    for h in range(C_HEADS):
        sl = slice(h * C_V_DIM, (h + 1) * C_V_DIM)
        hh = (hf_ref[0, :, sl] + hb_ref[0, :, sl]) * og_ref[0, :, sl].astype(F32)
        hh = _rmsnorm_rows(hh, gh_ref[:, sl])
        parts.append((hh * gz_ref[0, :, sl].astype(F32)).astype(BF16))
    y = _dot(jnp.concatenate(parts, axis=1), w_ref[...])
    o_ref[0] = x_ref[0] + gate_ref[0] * _rmsnorm_rows(y, g_ref[...])


def _out_proj_c(hf, hb, og, gz, g_head, w_out, x_all, gate, g_post, n_batch):
    bsz, rows, d = x_all.shape
    mod_map = lambda b, t: (jnp.where(t == 0, n_batch, b), 0, 0)
    return pl.pallas_call(
        _out_c_kernel,
        grid=(bsz, rows // ROW_TILE),
        in_specs=[_rows(C_WIDTH)] * 4 + [_whole(g_head.shape), _whole(w_out.shape), _rows(d),
                                          pl.BlockSpec((1, 1, d), mod_map), _whole(g_post.shape)],
        out_specs=_rows(d),
        out_shape=jax.ShapeDtypeStruct((bsz, rows, d), F32),
        compiler_params=_cparams(("parallel", "parallel")),
        name="out_proj_mlstm",
    )(hf, hb, og, gz, g_head, w_out, x_all, gate, g_post)


def _heads_cols(w, n_heads, head_dim, order=None, deinterleave=False):
    k = w.shape[0]
    w = w.reshape(k, n_heads, head_dim)
    if order is not None:
        w = w[:, jnp.asarray(order), :]
    if deinterleave:
        w = w.reshape(k, n_heads, head_dim // 2, 2).swapaxes(2, 3)
    return w.reshape(k, n_heads * head_dim)


def _rope_tables(n_lat, n_ctx):
    t = jnp.arange(n_lat)
    r = (t // GRID_W).astype(F32)
    c = (t % GRID_W).astype(F32)
    n_freq = 16
    inv = ROPE_BASE ** (-jnp.arange(n_freq, dtype=F32) / n_freq)
    ang = jnp.concatenate([r[:, None] * inv, c[:, None] * inv], axis=-1)
    cos, sin = jnp.cos(ang), jnp.sin(ang)
    cos_t = jnp.concatenate([cos] * 4, axis=-1)
    sin_t = jnp.concatenate([-sin, sin, -sin, sin], axis=-1)
    cos_t = jnp.concatenate([jnp.ones((n_ctx, LANES), F32), cos_t], axis=0)
    sin_t = jnp.concatenate([jnp.zeros((n_ctx, LANES), F32), sin_t], axis=0)
    return cos_t, sin_t


def _prep_a(w_in, sink, w_out):
    cuts = (A_WIDTH, A_WIDTH + A_KV_WIDTH, A_WIDTH + 2 * A_KV_WIDTH)
    wq = _heads_cols(w_in[:, :cuts[0]], A_HEADS, A_HEAD_DIM, A_HEAD_ORDER, True)
    wk = _heads_cols(w_in[:, cuts[0]:cuts[1]], A_KV_HEADS, A_HEAD_DIM, None, True)
    wz = _heads_cols(w_in[:, cuts[2]:], A_HEADS, A_HEAD_DIM, A_HEAD_ORDER)
    w = jnp.concatenate([wq, wk, w_in[:, cuts[1]:cuts[2]], wz], axis=1).astype(BF16)
    wo = _heads_cols(w_out.T, A_HEADS, A_HEAD_DIM, A_HEAD_ORDER).T.astype(BF16)
    return w, sink.astype(F32), wo


def _prep_b(w_in, w_uq, w_ukv):
    c1, c2, c3 = B_Q_RANK, B_Q_RANK + B_KV_RANK, B_Q_RANK + B_KV_RANK + B_ROPE
    kpe = _heads_cols(w_in[:, c2:c3], 1, B_ROPE, None, True)
    w = jnp.concatenate([w_in[:, :c2], kpe, kpe, w_in[:, c3:]], axis=1).astype(BF16)
    dqk = B_NOPE + B_ROPE
    wuq = w_uq.reshape(B_Q_RANK, B_HEADS, dqk)
    wukv = w_ukv.reshape(B_KV_RANK, B_HEADS, B_NOPE + B_VDIM)
    absorbed = _absorb_weights(jnp.moveaxis(wuq[:, :, :B_NOPE], 1, 0),
                               jnp.moveaxis(wukv[:, :, :B_NOPE], 1, 0))
    wq_nope = jnp.moveaxis(absorbed, 0, 1).reshape(B_Q_RANK, B_HEADS * B_KV_RANK)
    wq_rope = _heads_cols(wuq[:, :, B_NOPE:].reshape(B_Q_RANK, B_HEADS * B_ROPE),
                          B_HEADS, B_ROPE, None, True)
    wq = jnp.concatenate([wq_nope, wq_rope], axis=1).astype(BF16)
    wv = jnp.moveaxis(wukv[:, :, B_NOPE:], 1, 0).astype(BF16)
    return w, wq, wv


def _prep_c(w_in):
    pad = LANES - 4 * C_HEADS
    return jnp.concatenate([w_in, jnp.zeros((w_in.shape[0], pad), w_in.dtype)], axis=1).astype(BF16)


def kernel(x, c, ctx, c_ctx,
           l0_w_ada, l0_b_ada, l0_g_pre, l0_g_post, l0_w_in, l0_sink, l0_w_out,
           l1_w_ada, l1_b_ada, l1_g_pre, l1_g_post, l1_w_in, l1_g_qa, l1_g_kva, l1_w_uq, l1_w_ukv, l1_w_out,
           l2_w_ada, l2_b_ada, l2_g_pre, l2_g_post, l2_w_in, l2_conv, l2_b_gate, l2_g_head, l2_w_out,
           l3_w_ada, l3_b_ada, l3_g_pre, l3_g_post, l3_w_in, l3_sink, l3_w_out):
    bsz, n_lat, d = x.shape
    n_ctx = ctx.shape[1]
    assert n_ctx == ROW_TILE and n_lat % MLA_KEY_TILE == 0 and bsz < SUBLANES
    x_all = jnp.concatenate([ctx, x], axis=1)
    cvec = jnp.zeros((SUBLANES, d), F32).at[:bsz].set(c).at[bsz].set(c_ctx)
    cos_t, sin_t = _rope_tables(n_lat, n_ctx)
    row = lambda g: g.reshape(1, -1).astype(F32)

    def mods(w_ada, b_ada):
        m = _modulation(cvec, w_ada, b_ada)
        return tuple(m[:, i * d:(i + 1) * d].reshape(SUBLANES, 1, d) for i in range(3))

    def gqa_layer(x_all, w_ada, b_ada, g_pre, g_post, w_in, sink, w_out, last):
        shift, scale, gate = mods(w_ada, b_ada)
        w, sink, wo = _prep_a(w_in, sink, w_out)
        q, k, v, gz = _in_proj_a(x_all, row(g_pre), scale, shift, w, cos_t, sin_t, bsz)
        act = _attention_a(sink, q, k, v, gz, n_ctx)
        return _out_proj(act, wo, x_all, gate, row(g_post), bsz, last)

    x_all = gqa_layer(x_all, l0_w_ada, l0_b_ada, l0_g_pre, l0_g_post, l0_w_in, l0_sink, l0_w_out, False)

    shift, scale, gate = mods(l1_w_ada, l1_b_ada)
    w, wq, wv = _prep_b(l1_w_in, l1_w_uq, l1_w_ukv)
    q, k, gz = _in_proj_b(x_all, row(l1_g_pre), scale, shift, w, row(l1_g_qa), row(l1_g_kva), wq,
                          cos_t, sin_t, bsz)
    act = _attention_b(q, k, wv, gz, n_ctx)
    x_all = _out_proj(act, l1_w_out.astype(BF16), x_all, gate, row(l1_g_post), bsz, False)

    shift, scale, gate = mods(l2_w_ada, l2_b_ada)
    qk_raw, v, og, gz, gates = _in_proj_c(x_all, row(l2_g_pre), scale, shift, _prep_c(l2_w_in),
                                          row(l2_b_gate), bsz)
    qk = _conv_qk(qk_raw, l2_conv.astype(F32))
    hf, hb = _mlstm(qk, v, gates, n_ctx)
    x_all = _out_proj_c(hf, hb, og, gz, row(l2_g_head), l2_w_out.astype(BF16), x_all, gate,
                        row(l2_g_post), bsz)

    return gqa_layer(x_all, l3_w_ada, l3_b_ada, l3_g_pre, l3_g_post, l3_w_in, l3_sink, l3_w_out, True)
```

```python
import functools

import jax
import jax.numpy as jnp
from jax import lax
from jax.experimental import pallas as pl
from jax.experimental.pallas import tpu as pltpu

F32 = jnp.float32
BF16 = jnp.bfloat16

GRID_W = 64
EPS = 1e-6
NEG_INF = -1e30
ROPE_BASE = 10000.0
LOG2_E = 1.4426950408889634
A_HEADS, A_KV_HEADS, A_HEAD_DIM = 16, 4, 64
A_WIDTH, A_KV_WIDTH = A_HEADS * A_HEAD_DIM, A_KV_HEADS * A_HEAD_DIM
B_HEADS, B_NOPE, B_ROPE, B_VDIM = 8, 128, 64, 128
B_Q_RANK, B_KV_RANK = 256, 128
B_WIDTH = B_HEADS * B_VDIM
C_HEADS, C_QK_DIM, C_V_DIM, C_CONV = 4, 128, 256, 5
C_QK_WIDTH, C_WIDTH = C_HEADS * C_QK_DIM, C_HEADS * C_V_DIM

LANES = 128
SUBLANES = 8
ROW_TILE = 256
ATT_BLOCK = 128
MLA_KEY_TILE = 512
VMEM_LIMIT = 56 * 1024 * 1024

A_HEAD_ORDER = tuple(
    (2 * g + half) * (A_HEADS // A_KV_HEADS) + i
    for g in range(A_KV_HEADS // 2) for i in range(A_HEADS // A_KV_HEADS) for half in range(2))


def _cparams(sem, flags=None):
    return pltpu.CompilerParams(dimension_semantics=sem, vmem_limit_bytes=VMEM_LIMIT, flags=flags)


def _silu(x):
    return x * jax.nn.sigmoid(x)


def _rmsnorm_rows(x, g):
    return x * lax.rsqrt(jnp.mean(x * x, axis=-1, keepdims=True) + EPS) * g


def _nt_dot(a, b):
    return lax.dot_general(a, b, (((1,), (1,)), ((), ())), preferred_element_type=F32)


def _dot(a, b):
    return jnp.dot(a, b, preferred_element_type=F32)


def _mod_kernel(c_ref, w_ref, b_ref, o_ref):
    s = _silu(c_ref[...])
    o_ref[...] = jnp.dot(s, w_ref[...], preferred_element_type=F32,
                         precision=lax.Precision.HIGHEST) + b_ref[...]


def _modulation(cvec, w_ada, b_ada):
    d = cvec.shape[1]
    n = w_ada.shape[1]
    return pl.pallas_call(
        _mod_kernel,
        grid=(n // d,),
        in_specs=[pl.BlockSpec((SUBLANES, d), lambda j: (0, 0)),
                  pl.BlockSpec((d, d), lambda j: (0, j)),
                  pl.BlockSpec((1, d), lambda j: (0, j))],
        out_specs=pl.BlockSpec((SUBLANES, d), lambda j: (0, j)),
        out_shape=jax.ShapeDtypeStruct((SUBLANES, n), F32),
        compiler_params=_cparams(("arbitrary",)),
        name="modulation",
    )(cvec, w_ada, b_ada.reshape(1, n))


def _prenorm(x_ref, g_ref, sc_ref, sh_ref):
    h = _rmsnorm_rows(x_ref[0], g_ref[...]) * (1.0 + sc_ref[0]) + sh_ref[0]
    return h.astype(BF16)


def _rope_group(xg, cos, sin_signed, low_half):
    swapped = jnp.where(low_half, pltpu.roll(xg, LANES - 32, 1), pltpu.roll(xg, 32, 1))
    return xg * cos + swapped * sin_signed


def _low_half_mask(rows):
    lane = lax.broadcasted_iota(jnp.int32, (rows, LANES), 1)
    return (lane & 63) < 32


def _row_specs(d, n_batch):
    mod_map = lambda b, t: (jnp.where(t == 0, n_batch, b), 0, 0)
    return [pl.BlockSpec((1, ROW_TILE, d), lambda b, t: (b, t, 0)),
            pl.BlockSpec((1, d), lambda b, t: (0, 0)),
            pl.BlockSpec((1, 1, d), mod_map),
            pl.BlockSpec((1, 1, d), mod_map)]


def _whole(shape):
    return pl.BlockSpec(shape, lambda b, t: (0,) * len(shape))


def _rows(width):
    return pl.BlockSpec((1, ROW_TILE, width), lambda b, t: (b, t, 0))


def _in_a_kernel(x_ref, g_ref, sc_ref, sh_ref, w_ref, cos_ref, sin_ref,
                 q_ref, k_ref, vt_ref, z_ref):
    hb = _prenorm(x_ref, g_ref, sc_ref, sh_ref)
    cos, sin = cos_ref[...], sin_ref[...]
    low = _low_half_mask(ROW_TILE)
    scale = A_HEAD_DIM ** -0.5 * LOG2_E
    q = _dot(hb, w_ref[:, :A_WIDTH])
    for j in range(A_WIDTH // LANES):
        sl = slice(j * LANES, (j + 1) * LANES)
        q_ref[0, :, sl] = (_rope_group(q[:, sl], cos, sin, low) * scale).astype(BF16)
    kv = _dot(hb, w_ref[:, A_WIDTH:A_WIDTH + 2 * A_KV_WIDTH])
    for j in range(A_KV_WIDTH // LANES):
        sl = slice(j * LANES, (j + 1) * LANES)
        k_ref[0, :, sl] = _rope_group(kv[:, sl], cos, sin, low).astype(BF16)
    vt_ref[0] = kv[:, A_KV_WIDTH:].T.astype(BF16)
    z = _dot(hb, w_ref[:, A_WIDTH + 2 * A_KV_WIDTH:])
    z_ref[0] = _silu(z).astype(BF16)


def _in_proj_a(x_all, g_pre, scale, shift, w, cos_t, sin_t, n_batch):
    bsz, rows, d = x_all.shape
    grid = (bsz, rows // ROW_TILE)
    tab = pl.BlockSpec((ROW_TILE, LANES), lambda b, t: (t, 0))
    out = lambda width: jax.ShapeDtypeStruct((bsz, rows, width), BF16)
    return pl.pallas_call(
        _in_a_kernel,
        grid=grid,
        in_specs=_row_specs(d, n_batch) + [_whole(w.shape), tab, tab],
        out_specs=[_rows(A_WIDTH), _rows(A_KV_WIDTH),
                   pl.BlockSpec((1, A_KV_WIDTH, ROW_TILE), lambda b, t: (b, 0, t)), _rows(A_WIDTH)],
        out_shape=[out(A_WIDTH), out(A_KV_WIDTH),
                   jax.ShapeDtypeStruct((bsz, A_KV_WIDTH, rows), BF16), out(A_WIDTH)],
        compiler_params=_cparams(("parallel", "parallel")),
        name="in_proj_gqa",
    )(x_all, g_pre, scale, shift, w, cos_t, sin_t)


def _attn_a_kernel(sink_ref, q_ref, kp_ref, kc_ref, kn_ref, kx_ref,
                   vp_ref, vc_ref, vn_ref, vx_ref, gz_ref, o_ref, *, n_blocks, ctx_blocks):
    blk = pl.program_id(1)
    is_lat = blk >= ctx_blocks
    far = 4 * ATT_BLOCK
    off_prev = jnp.where(blk > ctx_blocks, 0, far)
    off_cur = jnp.where(is_lat, 0, far)
    off_next = jnp.where(jnp.logical_and(is_lat, blk < n_blocks - 1), 0, far)
    grp = A_HEADS // A_KV_HEADS
    n_cols = 2 * grp * ATT_BLOCK
    key = lax.broadcasted_iota(jnp.int32, (ATT_BLOCK, n_cols), 0)
    qry = lax.broadcasted_iota(jnp.int32, (ATT_BLOCK, n_cols), 1) & (ATT_BLOCK - 1)
    ok_prev = key >= qry + off_prev
    ok_cur = key >= off_cur
    ok_next = key <= qry - off_next
    lane = lax.broadcasted_iota(jnp.int32, (ATT_BLOCK, LANES), 1)
    lane_lo = lane < A_HEAD_DIM
    sub_lo = lax.broadcasted_iota(jnp.int32, (LANES, ATT_BLOCK), 0) < A_HEAD_DIM
    keys = jnp.concatenate([kp_ref[0], kc_ref[0], kn_ref[0], kx_ref[0]], axis=0)
    vals_t = jnp.concatenate([vp_ref[0], vc_ref[0], vn_ref[0], vx_ref[0]], axis=1)
    zero = jnp.zeros((), BF16)
    for g in range(A_KV_HEADS // 2):
        kg = keys[:, g * LANES:(g + 1) * LANES]
        vg_t = vals_t[g * LANES:(g + 1) * LANES, :]
        parts = []
        for i in range(grp):
            qv = q_ref[0, :, (g * grp + i) * LANES:(g * grp + i + 1) * LANES]
            parts += [jnp.where(lane_lo, qv, zero), jnp.where(lane_lo, zero, qv)]
        st = _nt_dot(kg, jnp.concatenate(parts, axis=0))
        st = jnp.concatenate(
            [jnp.where(ok_prev, st[:ATT_BLOCK], NEG_INF),
             jnp.where(ok_cur, st[ATT_BLOCK:2 * ATT_BLOCK], NEG_INF),
             jnp.where(ok_next, st[2 * ATT_BLOCK:3 * ATT_BLOCK], NEG_INF),
             st[3 * ATT_BLOCK:]], axis=0)
        sink = jnp.concatenate(
            [jnp.full((1, ATT_BLOCK), sink_ref[A_HEAD_ORDER[2 * (g * grp + i) + half]] * LOG2_E, F32)
             for i in range(grp) for half in range(2)], axis=1)
        m = jnp.maximum(jnp.max(st, axis=0, keepdims=True), sink)
        e = jnp.exp2(st - m)
        den = jnp.sum(e, axis=0, keepdims=True) + jnp.exp2(sink - m)
        pv = _dot(vg_t, e.astype(BF16)) / den
        for i in range(grp):
            lo = pv[:, (2 * i) * ATT_BLOCK:(2 * i + 1) * ATT_BLOCK]
            hi = pv[:, (2 * i + 1) * ATT_BLOCK:(2 * i + 2) * ATT_BLOCK]
            o = jnp.where(sub_lo, lo, hi).T
            sl = slice((g * grp + i) * LANES, (g * grp + i + 1) * LANES)
            o_ref[0, :, sl] = (o * gz_ref[0, :, sl].astype(F32)).astype(BF16)


def _attention_a(sink, q, k, vt, gz, n_ctx):
    bsz, rows, _ = q.shape
    n_blocks = rows // ATT_BLOCK
    ctx_blocks = n_ctx // ATT_BLOCK
    lo, hi = ctx_blocks, n_blocks - 1
    k_block = lambda f: pl.BlockSpec((1, ATT_BLOCK, A_KV_WIDTH), lambda b, t: (b, f(t), 0))
    v_block = lambda f: pl.BlockSpec((1, A_KV_WIDTH, ATT_BLOCK), lambda b, t: (b, 0, f(t)))
    band = (lambda t: jnp.clip(t - 1, lo, hi), lambda t: jnp.clip(t, lo, hi), lambda t: jnp.clip(t + 1, lo, hi))
    k_ctx = pl.BlockSpec((1, n_ctx, A_KV_WIDTH), lambda b, t: (b, 0, 0))
    v_ctx = pl.BlockSpec((1, A_KV_WIDTH, n_ctx), lambda b, t: (b, 0, 0))
    qspec = pl.BlockSpec((1, ATT_BLOCK, A_WIDTH), lambda b, t: (b, t, 0))
    return pl.pallas_call(
        functools.partial(_attn_a_kernel, n_blocks=n_blocks, ctx_blocks=ctx_blocks),
        grid=(bsz, n_blocks),
        in_specs=[pl.BlockSpec(memory_space=pltpu.SMEM), qspec]
                 + [k_block(f) for f in band] + [k_ctx] + [v_block(f) for f in band] + [v_ctx, qspec],
        out_specs=qspec,
        out_shape=jax.ShapeDtypeStruct((bsz, rows, A_WIDTH), BF16),
        compiler_params=_cparams(("parallel", "parallel")),
        name="attention_gqa",
    )(sink, q, k, k, k, k, vt, vt, vt, vt, gz)


def _out_kernel(a_ref, w_ref, x_ref, gate_ref, g_ref, o_ref):
    y = _dot(a_ref[0], w_ref[...])
    o_ref[0] = x_ref[0] + gate_ref[0] * _rmsnorm_rows(y, g_ref[...])


def _out_proj(act, w_out, x_all, gate, g_post, n_batch, latent_only):
    bsz, rows, d = x_all.shape
    tiles = rows // ROW_TILE
    skip = 1 if latent_only else 0
    mod_map = lambda b, t: (jnp.where(t + skip == 0, n_batch, b), 0, 0)
    return pl.pallas_call(
        _out_kernel,
        grid=(bsz, tiles - skip),
        in_specs=[pl.BlockSpec((1, ROW_TILE, act.shape[2]), lambda b, t: (b, t + skip, 0)),
                  _whole(w_out.shape),
                  pl.BlockSpec((1, ROW_TILE, d), lambda b, t: (b, t + skip, 0)),
                  pl.BlockSpec((1, 1, d), mod_map),
                  pl.BlockSpec((1, d), lambda b, t: (0, 0))],
        out_specs=pl.BlockSpec((1, ROW_TILE, d), lambda b, t: (b, t, 0)),
        out_shape=jax.ShapeDtypeStruct((bsz, rows - skip * ROW_TILE, d), F32),
        compiler_params=_cparams(("parallel", "parallel")),
        name="out_proj",
    )(act, w_out, x_all, gate, g_post)


def _absorb_kernel(wq_ref, wk_ref, o_ref):
    o_ref[0] = lax.dot_general(wq_ref[0], wk_ref[0], (((1,), (1,)), ((), ())),
                               preferred_element_type=F32, precision=lax.Precision.HIGHEST)


def _absorb_weights(wq_nope, wk_nope):
    heads, q_rank, nope = wq_nope.shape
    kv_rank = wk_nope.shape[1]
    return pl.pallas_call(
        _absorb_kernel,
        grid=(heads,),
        in_specs=[pl.BlockSpec((1, q_rank, nope), lambda h: (h, 0, 0)),
                  pl.BlockSpec((1, kv_rank, nope), lambda h: (h, 0, 0))],
        out_specs=pl.BlockSpec((1, q_rank, kv_rank), lambda h: (h, 0, 0)),
        out_shape=jax.ShapeDtypeStruct((heads, q_rank, kv_rank), F32),
        compiler_params=_cparams(("arbitrary",)),
        name="mla_absorb",
    )(wq_nope, wk_nope)


def _in_b_kernel(x_ref, g_ref, sc_ref, sh_ref, w_ref, gq_ref, gk_ref, wq_ref, cos_ref, sin_ref,
                 q_ref, k_ref, vt_ref, z_ref):
    hb = _prenorm(x_ref, g_ref, sc_ref, sh_ref)
    cos, sin = cos_ref[...], sin_ref[...]
    low = _low_half_mask(ROW_TILE)
    lane = lax.broadcasted_iota(jnp.int32, (ROW_TILE, LANES), 1)
    n_a = B_Q_RANK + B_KV_RANK + LANES
    a = _dot(hb, w_ref[:, :n_a])
    z_ref[0] = _silu(_dot(hb, w_ref[:, n_a:])).astype(BF16)
    cn = _rmsnorm_rows(a[:, B_Q_RANK:B_Q_RANK + B_KV_RANK], gk_ref[...])
    k_ref[0, :, :LANES] = cn.astype(BF16)
    vt_ref[0] = cn.T.astype(BF16)
    k_ref[0, :, LANES:] = _rope_group(a[:, B_Q_RANK + B_KV_RANK:], cos, sin, low).astype(BF16)
    qn = _rmsnorm_rows(a[:, :B_Q_RANK], gq_ref[...]).astype(BF16)
    qq = _dot(qn, wq_ref[...]) * ((B_NOPE + B_ROPE) ** -0.5 * LOG2_E)
    nope_w = B_HEADS * B_KV_RANK
    for pair in range(B_HEADS // 2):
        rp = _rope_group(qq[:, nope_w + pair * LANES:nope_w + (pair + 1) * LANES], cos, sin, low)
        for half in range(2):
            h = 2 * pair + half
            q_ref[0, h, :, :LANES] = qq[:, h * B_KV_RANK:(h + 1) * B_KV_RANK].astype(BF16)
            own = (lane < B_ROPE) if half == 0 else (lane >= B_ROPE)
            q_ref[0, h, :, LANES:] = jnp.where(own, rp, 0.0).astype(BF16)


def _in_proj_b(x_all, g_pre, scale, shift, w, g_qa, g_kva, wq, cos_t, sin_t, n_batch):
    bsz, rows, d = x_all.shape
    tab = pl.BlockSpec((ROW_TILE, LANES), lambda b, t: (t, 0))
    return pl.pallas_call(
        _in_b_kernel,
        grid=(bsz, rows // ROW_TILE),
        in_specs=_row_specs(d, n_batch) + [_whole(w.shape), _whole(g_qa.shape), _whole(g_kva.shape),
                                           _whole(wq.shape), tab, tab],
        out_specs=[pl.BlockSpec((1, B_HEADS, ROW_TILE, 2 * LANES), lambda b, t: (b, 0, t, 0)),
                   _rows(2 * LANES),
                   pl.BlockSpec((1, B_KV_RANK, ROW_TILE), lambda b, t: (b, 0, t)),
                   _rows(B_WIDTH)],
        out_shape=[jax.ShapeDtypeStruct((bsz, B_HEADS, rows, 2 * LANES), BF16),
                   jax.ShapeDtypeStruct((bsz, rows, 2 * LANES), BF16),
                   jax.ShapeDtypeStruct((bsz, B_KV_RANK, rows), BF16),
                   jax.ShapeDtypeStruct((bsz, rows, B_WIDTH), BF16)],
        compiler_params=_cparams(("parallel", "parallel")),
        name="in_proj_mla",
    )(x_all, g_pre, scale, shift, w, g_qa, g_kva, wq, cos_t, sin_t)


def _attn_b_kernel(q_ref, k_ref, vt_ref, wv_ref, gz_ref, o_ref, m_ref, l_ref, acc_ref, s_ref, mx_ref,
                   *, n_ctx, n_lat):
    tile = pl.program_id(1)
    qs = q_ref[0].reshape(B_HEADS * ROW_TILE, 2 * LANES)
    n_tiles = n_lat // MLA_KEY_TILE

    def key_start(j):
        return pl.multiple_of(n_ctx + j * MLA_KEY_TILE, LANES)

    def scores(kt):
        return _nt_dot(kt, qs)

    def accumulate(st, mx, vt, first):
        m_new = mx if first else jnp.maximum(m_ref[...], mx)
        et = jnp.exp2(st - m_new)
        ls = jnp.sum(et, axis=0, keepdims=True)
        pv = _dot(vt, et.astype(BF16))
        if first:
            l_ref[...] = ls
            acc_ref[...] = pv
        else:
            alpha = jnp.exp2(m_ref[...] - m_new)
            l_ref[...] = alpha * l_ref[...] + ls
            acc_ref[...] = alpha * acc_ref[...] + pv
        m_ref[...] = m_new

    def stash(slot, j):
        st = scores(k_ref[0, pl.ds(key_start(j), MLA_KEY_TILE), :])
        s_ref[slot] = st
        mx_ref[slot] = jnp.max(st, axis=0, keepdims=True)

    def consume(slot, j):
        accumulate(s_ref[slot], mx_ref[slot], vt_ref[0, :, pl.ds(key_start(j), MLA_KEY_TILE)], False)

    st0 = scores(k_ref[0, :n_ctx, :])
    accumulate(st0, jnp.max(st0, axis=0, keepdims=True), vt_ref[0, :, :n_ctx], True)

    @pl.when(tile != 0)
    def _():
        stash(0, 0)

        def body(i, carry):
            stash(1, 2 * i + 1)
            consume(0, 2 * i)
            stash(0, 2 * i + 2)
            consume(1, 2 * i + 1)
            return carry

        lax.fori_loop(0, n_tiles // 2 - 1, body, 0)
        stash(1, n_tiles - 1)
        consume(0, n_tiles - 2)
        consume(1, n_tiles - 1)

    ot = acc_ref[...] / l_ref[...]
    for h in range(B_HEADS):
        oh = _dot(ot[:, h * ROW_TILE:(h + 1) * ROW_TILE].T.astype(BF16), wv_ref[h])
        sl = slice(h * B_VDIM, (h + 1) * B_VDIM)
        o_ref[0, :, sl] = (oh * gz_ref[0, :, sl].astype(F32)).astype(BF16)


def _attention_b(q, k, vt, wv, gz, n_ctx):
    bsz, heads, rows, qw = q.shape
    n_cols = heads * ROW_TILE
    return pl.pallas_call(
        functools.partial(_attn_b_kernel, n_ctx=n_ctx, n_lat=rows - n_ctx),
        grid=(bsz, rows // ROW_TILE),
        in_specs=[pl.BlockSpec((1, heads, ROW_TILE, qw), lambda b, t: (b, 0, t, 0)),
                  pl.BlockSpec((1, rows, qw), lambda b, t: (b, 0, 0)),
                  pl.BlockSpec((1, B_KV_RANK, rows), lambda b, t: (b, 0, 0)),
                  _whole(wv.shape), _rows(B_WIDTH)],
        out_specs=_rows(B_WIDTH),
        out_shape=jax.ShapeDtypeStruct((bsz, rows, B_WIDTH), BF16),
        scratch_shapes=[pltpu.VMEM((1, n_cols), F32), pltpu.VMEM((1, n_cols), F32),
                        pltpu.VMEM((B_KV_RANK, n_cols), F32),
                        pltpu.VMEM((2, MLA_KEY_TILE, n_cols), F32), pltpu.VMEM((2, 1, n_cols), F32)],
        compiler_params=_cparams(("parallel", "arbitrary")),
        name="attention_mla",
    )(q, k, vt, wv, gz)


def _in_c_kernel(x_ref, g_ref, sc_ref, sh_ref, w_ref, bg_ref, qk_ref, v_ref, o_ref, z_ref, gt_ref):
    hb = _prenorm(x_ref, g_ref, sc_ref, sh_ref)
    qk_ref[0] = _dot(hb, w_ref[:, :2 * C_QK_WIDTH])
    c0 = 2 * C_QK_WIDTH
    v_ref[0] = _dot(hb, w_ref[:, c0:c0 + C_WIDTH]).astype(BF16)
    o_ref[0] = jax.nn.sigmoid(_dot(hb, w_ref[:, c0 + C_WIDTH:c0 + 2 * C_WIDTH])).astype(BF16)
    z_ref[0] = _silu(_dot(hb, w_ref[:, c0 + 2 * C_WIDTH:c0 + 3 * C_WIDTH])).astype(BF16)
    gates = _dot(hb, w_ref[:, c0 + 3 * C_WIDTH:])[:, :4 * C_HEADS] + bg_ref[...]
    col = lax.broadcasted_iota(jnp.int32, gates.shape, 1)
    is_forget = ((col >> 2) & 1) == 1
    log_sig = jnp.minimum(gates, 0.0) - jnp.log1p(jnp.exp(-jnp.abs(gates)))
    gt_ref[0] = jnp.where(is_forget, log_sig, gates)


def _in_proj_c(x_all, g_pre, scale, shift, w, b_gate, n_batch):
    bsz, rows, d = x_all.shape
    out = lambda width, dt: jax.ShapeDtypeStruct((bsz, rows, width), dt)
    ng = 4 * C_HEADS
    return pl.pallas_call(
        _in_c_kernel,
        grid=(bsz, rows // ROW_TILE),
        in_specs=_row_specs(d, n_batch) + [_whole(w.shape), _whole(b_gate.shape)],
        out_specs=[_rows(2 * C_QK_WIDTH), _rows(C_WIDTH), _rows(C_WIDTH), _rows(C_WIDTH), _rows(ng)],
        out_shape=[out(2 * C_QK_WIDTH, F32), out(C_WIDTH, BF16), out(C_WIDTH, BF16),
                   out(C_WIDTH, BF16), out(ng, F32)],
        compiler_params=_cparams(("parallel", "parallel")),
        name="in_proj_mlstm",
    )(x_all, g_pre, scale, shift, w, b_gate)


def _conv_kernel(cur_ref, prev_ref, next_ref, w_ref, o_ref, buf_ref, *, n_tiles):
    tile = pl.program_id(1)
    pad = SUBLANES
    half = C_CONV // 2
    has_prev = (tile >= 2).astype(F32)
    has_next = jnp.logical_and(tile >= 1, tile < n_tiles - 1).astype(F32)
    buf_ref[:pad] = prev_ref[0] * has_prev
    buf_ref[pad:pad + ROW_TILE] = cur_ref[0]
    buf_ref[pad + ROW_TILE:] = next_ref[0] * has_next
    acc = None
    for j in range(C_CONV):
        term = buf_ref[pl.ds(pad - half + j, ROW_TILE), :] * w_ref[j:j + 1, :]
        acc = term if acc is None else acc + term
    act = _silu(acc)
    o_ref[0, :, :C_QK_WIDTH] = act[:, :C_QK_WIDTH].astype(BF16)
    o_ref[0, :, C_QK_WIDTH:] = (act[:, C_QK_WIDTH:] * (C_QK_DIM ** -0.5)).astype(BF16)


def _conv_qk(qk_raw, conv_w):
    bsz, rows, width = qk_raw.shape
    n_tiles = rows // ROW_TILE
    per = ROW_TILE // SUBLANES
    last = rows // SUBLANES - 1
    return pl.pallas_call(
        functools.partial(_conv_kernel, n_tiles=n_tiles),
        grid=(bsz, n_tiles),
        in_specs=[_rows(width),
                  pl.BlockSpec((1, SUBLANES, width), lambda b, t: (b, jnp.maximum(t * per - 1, 0), 0)),
                  pl.BlockSpec((1, SUBLANES, width), lambda b, t: (b, jnp.minimum((t + 1) * per, last), 0)),
                  _whole(conv_w.shape)],
        out_specs=_rows(width),
        out_shape=jax.ShapeDtypeStruct((bsz, rows, width), BF16),
        scratch_shapes=[pltpu.VMEM((ROW_TILE + 2 * SUBLANES, width), F32)],
        compiler_params=_cparams(("parallel", "parallel")),
        name="conv_qk",
    )(qk_raw, qk_raw, qk_raw, conv_w)


def _mlstm_chunk(q, k, v, ig_col, lf_col, ig_row, lf_row, ct_ref, n_ref, m_ref, head, reverse):
    L = ATT_BLOCK
    t_idx = lax.broadcasted_iota(jnp.int32, (L, L), 0)
    s_idx = lax.broadcasted_iota(jnp.int32, (L, L), 1)
    seen = (s_idx >= t_idx) if reverse else (s_idx <= t_idx)
    cum_col = jnp.sum(jnp.where(seen, lf_row, 0.0), axis=1, keepdims=True)
    cum_row = jnp.sum(jnp.where(seen, 0.0, lf_col) + jnp.where(s_idx == t_idx, lf_col, 0.0),
                      axis=0, keepdims=True)
    tot = jnp.sum(lf_row, axis=1, keepdims=True)
    m_st = m_ref[head]
    ct = ct_ref[head]
    n_st = n_ref[head]

    log_d = jnp.where(seen, cum_col - cum_row + ig_row, NEG_INF)
    inter = cum_col + m_st
    m_t = jnp.maximum(inter, jnp.max(log_d, axis=1, keepdims=True))
    s_qk = _nt_dot(q, k) * jnp.exp(log_d - m_t)
    a_t = jnp.exp(inter - m_t)
    qf = q.astype(F32)
    num = _dot(s_qk.astype(BF16), v) + a_t * _dot(q, ct.astype(BF16))
    den = jnp.sum(s_qk, axis=1, keepdims=True) + a_t * jnp.sum(qf * n_st, axis=1, keepdims=True)
    out = num / jnp.maximum(jnp.abs(den), jnp.exp(-m_t))

    w_end = tot - cum_col + ig_col
    m_loc = jnp.max(w_end, axis=0, keepdims=True)
    e_end = jnp.exp(w_end - m_loc)
    kf = k.astype(F32)
    c_loc = _dot(kf.T.astype(BF16), (v.astype(F32) * e_end).astype(BF16))
    n_loc = jnp.sum(kf * e_end, axis=0, keepdims=True)
    m_new = jnp.maximum(tot + m_st, m_loc)
    a = jnp.exp(tot + m_st - m_new)
    bb = jnp.exp(m_loc - m_new)
    ct_ref[head] = a * ct + bb * c_loc
    n_ref[head] = a * n_st + bb * n_loc
    m_ref[head] = m_new
    return out


def _mlstm_kernel(qkf_ref, vf_ref, gcf_ref, grf_ref, qkb_ref, vb_ref, gcb_ref, grb_ref,
                  hf_ref, hb_ref, ctf_ref, nf_ref, mf_ref, ctb_ref, nb_ref, mb_ref):
    @pl.when(pl.program_id(1) == 0)
    def _():
        for ct, n, m in ((ctf_ref, nf_ref, mf_ref), (ctb_ref, nb_ref, mb_ref)):
            ct[...] = jnp.zeros(ct.shape, F32)
            n[...] = jnp.zeros(n.shape, F32)
            m[...] = jnp.full(m.shape, NEG_INF, F32)

    dirs = ((qkf_ref, vf_ref, gcf_ref, grf_ref, hf_ref, ctf_ref, nf_ref, mf_ref, False),
            (qkb_ref, vb_ref, gcb_ref, grb_ref, hb_ref, ctb_ref, nb_ref, mb_ref, True))
    for d, (qk, v, gc, gr, out, ct, n, m, reverse) in enumerate(dirs):
        for h in range(C_HEADS):
            i_idx = (2 * d) * C_HEADS + h
            f_idx = (2 * d + 1) * C_HEADS + h
            out[0, :, h * C_V_DIM:(h + 1) * C_V_DIM] = _mlstm_chunk(
                qk[0, :, h * C_QK_DIM:(h + 1) * C_QK_DIM],
                qk[0, :, C_QK_WIDTH + h * C_QK_DIM:C_QK_WIDTH + (h + 1) * C_QK_DIM],
                v[0, :, h * C_V_DIM:(h + 1) * C_V_DIM],
                gc[0, :, i_idx:i_idx + 1], gc[0, :, f_idx:f_idx + 1],
                gr[0, i_idx:i_idx + 1, :], gr[0, f_idx:f_idx + 1, :],
                ct, n, m, h, reverse)


def _mlstm(qk, v, gates, n_ctx):
    bsz, rows, _ = qk.shape
    n_chunks = rows // ATT_BLOCK
    ctx_chunks = n_ctx // ATT_BLOCK
    gates_t = jnp.swapaxes(gates, 1, 2)
    ng = gates.shape[2]
    fwd = lambda g: g
    bwd = lambda g: jnp.where(g < ctx_chunks, ctx_chunks - 1 - g, n_chunks - 1 + ctx_chunks - g)

    def specs(order):
        return [pl.BlockSpec((1, ATT_BLOCK, 2 * C_QK_WIDTH), lambda b, g: (b, order(g), 0)),
                pl.BlockSpec((1, ATT_BLOCK, C_WIDTH), lambda b, g: (b, order(g), 0)),
                pl.BlockSpec((1, ATT_BLOCK, ng), lambda b, g: (b, order(g), 0)),
                pl.BlockSpec((1, ng, ATT_BLOCK), lambda b, g: (b, 0, order(g)))]

    hspec = lambda order: pl.BlockSpec((1, ATT_BLOCK, C_WIDTH), lambda b, g: (b, order(g), 0))
    state = [pltpu.VMEM((C_HEADS, C_QK_DIM, C_V_DIM), F32), pltpu.VMEM((C_HEADS, 1, C_QK_DIM), F32),
             pltpu.VMEM((C_HEADS, 1, 1), F32)]
    return pl.pallas_call(
        _mlstm_kernel,
        grid=(bsz, n_chunks),
        in_specs=specs(fwd) + specs(bwd),
        out_specs=[hspec(fwd), hspec(bwd)],
        out_shape=[jax.ShapeDtypeStruct((bsz, rows, C_WIDTH), F32)] * 2,
        scratch_shapes=state + state,
        compiler_params=_cparams(("arbitrary", "arbitrary")),
        name="mlstm_scan",
    )(qk, v, gates, gates_t, qk, v, gates, gates_t)


def _out_c_kernel(hf_ref, hb_ref, og_ref, gz_ref, gh_ref, w_ref, x_ref, gate_ref, g_ref, o_ref):
    parts = []
    for h in range(C_HEADS):
        sl = slice(h * C_V_DIM, (h + 1) * C_V_DIM)
        hh = (hf_ref[0, :, sl] + hb_ref[0, :, sl]) * og_ref[0, :, sl].astype(F32)
        hh = _rmsnorm_rows(hh, gh_ref[:, sl])
        parts.append((hh * gz_ref[0, :, sl].astype(F32)).astype(BF16))
    y = _dot(jnp.concatenate(parts, axis=1), w_ref[...])
    o_ref[0] = x_ref[0] + gate_ref[0] * _rmsnorm_rows(y, g_ref[...])


def _out_proj_c(hf, hb, og, gz, g_head, w_out, x_all, gate, g_post, n_batch):
    bsz, rows, d = x_all.shape
    mod_map = lambda b, t: (jnp.where(t == 0, n_batch, b), 0, 0)
    return pl.pallas_call(
        _out_c_kernel,
        grid=(bsz, rows // ROW_TILE),
        in_specs=[_rows(C_WIDTH)] * 4 + [_whole(g_head.shape), _whole(w_out.shape), _rows(d),
                                          pl.BlockSpec((1, 1, d), mod_map), _whole(g_post.shape)],
        out_specs=_rows(d),
        out_shape=jax.ShapeDtypeStruct((bsz, rows, d), F32),
        compiler_params=_cparams(("parallel", "parallel")),
        name="out_proj_mlstm",
    )(hf, hb, og, gz, g_head, w_out, x_all, gate, g_post)


def _heads_cols(w, n_heads, head_dim, order=None, deinterleave=False):
    k = w.shape[0]
    w = w.reshape(k, n_heads, head_dim)
    if order is not None:
        w = w[:, jnp.asarray(order), :]
    if deinterleave:
        w = w.reshape(k, n_heads, head_dim // 2, 2).swapaxes(2, 3)
    return w.reshape(k, n_heads * head_dim)


def _rope_tables(n_lat, n_ctx):
    t = jnp.arange(n_lat)
    r = (t // GRID_W).astype(F32)
    c = (t % GRID_W).astype(F32)
    n_freq = 16
    inv = ROPE_BASE ** (-jnp.arange(n_freq, dtype=F32) / n_freq)
    ang = jnp.concatenate([r[:, None] * inv, c[:, None] * inv], axis=-1)
    cos, sin = jnp.cos(ang), jnp.sin(ang)
    cos_t = jnp.concatenate([cos] * 4, axis=-1)
    sin_t = jnp.concatenate([-sin, sin, -sin, sin], axis=-1)
    cos_t = jnp.concatenate([jnp.ones((n_ctx, LANES), F32), cos_t], axis=0)
    sin_t = jnp.concatenate([jnp.zeros((n_ctx, LANES), F32), sin_t], axis=0)
    return cos_t, sin_t


def _prep_a(w_in, sink, w_out):
    cuts = (A_WIDTH, A_WIDTH + A_KV_WIDTH, A_WIDTH + 2 * A_KV_WIDTH)
    wq = _heads_cols(w_in[:, :cuts[0]], A_HEADS, A_HEAD_DIM, A_HEAD_ORDER, True)
    wk = _heads_cols(w_in[:, cuts[0]:cuts[1]], A_KV_HEADS, A_HEAD_DIM, None, True)
    wz = _heads_cols(w_in[:, cuts[2]:], A_HEADS, A_HEAD_DIM, A_HEAD_ORDER)
    w = jnp.concatenate([wq, wk, w_in[:, cuts[1]:cuts[2]], wz], axis=1).astype(BF16)
    wo = _heads_cols(w_out.T, A_HEADS, A_HEAD_DIM, A_HEAD_ORDER).T.astype(BF16)
    return w, sink.astype(F32), wo


def _prep_b(w_in, w_uq, w_ukv):
    c1, c2, c3 = B_Q_RANK, B_Q_RANK + B_KV_RANK, B_Q_RANK + B_KV_RANK + B_ROPE
    kpe = _heads_cols(w_in[:, c2:c3], 1, B_ROPE, None, True)
    w = jnp.concatenate([w_in[:, :c2], kpe, kpe, w_in[:, c3:]], axis=1).astype(BF16)
    dqk = B_NOPE + B_ROPE
    wuq = w_uq.reshape(B_Q_RANK, B_HEADS, dqk)
    wukv = w_ukv.reshape(B_KV_RANK, B_HEADS, B_NOPE + B_VDIM)
    absorbed = _absorb_weights(jnp.moveaxis(wuq[:, :, :B_NOPE], 1, 0),
                               jnp.moveaxis(wukv[:, :, :B_NOPE], 1, 0))
    wq_nope = jnp.moveaxis(absorbed, 0, 1).reshape(B_Q_RANK, B_HEADS * B_KV_RANK)
    wq_rope = _heads_cols(wuq[:, :, B_NOPE:].reshape(B_Q_RANK, B_HEADS * B_ROPE),
                          B_HEADS, B_ROPE, None, True)
    wq = jnp.concatenate([wq_nope, wq_rope], axis=1).astype(BF16)
    wv = jnp.moveaxis(wukv[:, :, B_NOPE:], 1, 0).astype(BF16)
    return w, wq, wv


def _prep_c(w_in):
    pad = LANES - 4 * C_HEADS
    return jnp.concatenate([w_in, jnp.zeros((w_in.shape[0], pad), w_in.dtype)], axis=1).astype(BF16)


def kernel(x, c, ctx, c_ctx,
           l0_w_ada, l0_b_ada, l0_g_pre, l0_g_post, l0_w_in, l0_sink, l0_w_out,
           l1_w_ada, l1_b_ada, l1_g_pre, l1_g_post, l1_w_in, l1_g_qa, l1_g_kva, l1_w_uq, l1_w_ukv, l1_w_out,
           l2_w_ada, l2_b_ada, l2_g_pre, l2_g_post, l2_w_in, l2_conv, l2_b_gate, l2_g_head, l2_w_out,
           l3_w_ada, l3_b_ada, l3_g_pre, l3_g_post, l3_w_in, l3_sink, l3_w_out):
    bsz, n_lat, d = x.shape
    n_ctx = ctx.shape[1]
    assert n_ctx == ROW_TILE and n_lat % (2 * MLA_KEY_TILE) == 0 and bsz < SUBLANES
    x_all = jnp.concatenate([ctx, x], axis=1)
    cvec = jnp.zeros((SUBLANES, d), F32).at[:bsz].set(c).at[bsz].set(c_ctx)
    cos_t, sin_t = _rope_tables(n_lat, n_ctx)
    row = lambda g: g.reshape(1, -1).astype(F32)

    def mods(w_ada, b_ada):
        m = _modulation(cvec, w_ada, b_ada)
        return tuple(m[:, i * d:(i + 1) * d].reshape(SUBLANES, 1, d) for i in range(3))

    def gqa_layer(x_all, w_ada, b_ada, g_pre, g_post, w_in, sink, w_out, last):
        shift, scale, gate = mods(w_ada, b_ada)
        w, sink, wo = _prep_a(w_in, sink, w_out)
        q, k, vt, gz = _in_proj_a(x_all, row(g_pre), scale, shift, w, cos_t, sin_t, bsz)
        act = _attention_a(sink, q, k, vt, gz, n_ctx)
        return _out_proj(act, wo, x_all, gate, row(g_post), bsz, last)

    x_all = gqa_layer(x_all, l0_w_ada, l0_b_ada, l0_g_pre, l0_g_post, l0_w_in, l0_sink, l0_w_out, False)

    shift, scale, gate = mods(l1_w_ada, l1_b_ada)
    w, wq, wv = _prep_b(l1_w_in, l1_w_uq, l1_w_ukv)
    q, k, vt, gz = _in_proj_b(x_all, row(l1_g_pre), scale, shift, w, row(l1_g_qa), row(l1_g_kva), wq,
                              cos_t, sin_t, bsz)
    act = _attention_b(q, k, vt, wv, gz, n_ctx)
    x_all = _out_proj(act, l1_w_out.astype(BF16), x_all, gate, row(l1_g_post), bsz, False)

    shift, scale, gate = mods(l2_w_ada, l2_b_ada)
    qk_raw, v, og, gz, gates = _in_proj_c(x_all, row(l2_g_pre), scale, shift, _prep_c(l2_w_in),
                                          row(l2_b_gate), bsz)
    qk = _conv_qk(qk_raw, l2_conv.astype(F32))
    hf, hb = _mlstm(qk, v, gates, n_ctx)
    x_all = _out_proj_c(hf, hb, og, gz, row(l2_g_head), l2_w_out.astype(BF16), x_all, gate,
                        row(l2_g_post), bsz)

    return gqa_layer(x_all, l3_w_ada, l3_b_ada, l3_g_pre, l3_g_post, l3_w_in, l3_sink, l3_w_out, True)
```
